```python
import math
import jax, jax.numpy as jnp
from jax import lax
import numpy as np

D_MODEL = 1024
BATCH = 2
SEQ = 8192
DEPTH = 4

HEAD_DIM = 64
A_HEADS = 4
A_CONFIGS = ((128, 1), (512, 4), (2048, 16))
A_ROPE_DIMS = HEAD_DIM // 4
ROPE_THETA = 500000.0
B_Q_HEADS = 8
B_KV_HEADS = 2
B_AXIAL_THETA = 10000.0
C_HEADS = 4
C_ROWS_MAX = 8
C_COLS = 16
GRID_W = 64
Q_BLOCK = 128
D_FF = 4 * D_MODEL
A_WIDTH = A_HEADS * HEAD_DIM
B_Q_WIDTH = B_Q_HEADS * HEAD_DIM
B_KV_WIDTH = B_KV_HEADS * HEAD_DIM
C_WIDTH = C_HEADS * HEAD_DIM
N_BRANCHES = 3
QKV_COLS = 3 * A_WIDTH + B_Q_WIDTH + 2 * B_KV_WIDTH + 3 * C_WIDTH
IN_COLS = QKV_COLS + N_BRANCHES * D_MODEL
DEEPNORM_ALPHA = (2 * DEPTH) ** 0.25
DEEPNORM_BETA = (8 * DEPTH) ** -0.25
LN_EPS = 1e-5
RMS_EPS = 1e-6
NEG_INF = -1e30

kernel_name = "hybrid_dilated_axial_neighbourhood_encoder"


def _split_points():
    sizes = [A_WIDTH, A_WIDTH, A_WIDTH, B_Q_WIDTH, B_KV_WIDTH, B_KV_WIDTH,
             C_WIDTH, C_WIDTH, C_WIDTH]
    return tuple(int(v) for v in np.cumsum(sizes))


def layer_norm(x, g, b):
    xf = x.astype(jnp.float32)
    mu = jnp.mean(xf, -1, keepdims=True)
    var = jnp.mean(jnp.square(xf - mu), -1, keepdims=True)
    y = (xf - mu) * lax.rsqrt(var + LN_EPS)
    return (y * g.astype(jnp.float32) + b.astype(jnp.float32)).astype(x.dtype)


def rms_norm(x, g):
    xf = x.astype(jnp.float32)
    y = xf * lax.rsqrt(jnp.mean(jnp.square(xf), -1, keepdims=True) + RMS_EPS)
    return (y * g.astype(jnp.float32)).astype(x.dtype)


def rotary(x, pos, theta):
    half = x.shape[-1] // 2
    inv = theta ** (-jnp.arange(half, dtype=jnp.float32) / half)
    ang = pos.astype(jnp.float32)[:, None] * inv[None, :]
    cos = jnp.cos(ang)[None, :, None, :]
    sin = jnp.sin(ang)[None, :, None, :]
    xf = x.astype(jnp.float32)
    x1, x2 = xf[..., :half], xf[..., half:]
    return jnp.concatenate([x1 * cos - x2 * sin, x2 * cos + x1 * sin], -1).astype(x.dtype)


def partial_rotary(x, pos):
    return jnp.concatenate([rotary(x[..., :A_ROPE_DIMS], pos, ROPE_THETA),
                            x[..., A_ROPE_DIMS:]], -1)


def axial_rotary(x, row, col):
    half = x.shape[-1] // 2
    return jnp.concatenate([rotary(x[..., :half], row, B_AXIAL_THETA),
                            rotary(x[..., half:], col, B_AXIAL_THETA)], -1)


def banded_window_stats(q, k, v, radius):
    L, hd = q.shape[-2], q.shape[-1]
    lead = q.shape[:-2]
    nb = -(-L // Q_BLOCK)
    lp = nb * Q_BLOCK
    pad_q = [(0, 0)] * len(lead) + [(0, lp - L), (0, 0)]
    pad_kv = [(0, 0)] * len(lead) + [(radius, lp - L + radius), (0, 0)]
    qb = jnp.pad(q, pad_q).reshape(lead + (nb, Q_BLOCK, hd))
    kp = jnp.pad(k, pad_kv)
    vp = jnp.pad(v, pad_kv)
    span = Q_BLOCK + 2 * radius
    idx = jnp.arange(nb)[:, None] * Q_BLOCK + jnp.arange(span)[None, :]
    kb = kp[..., idx, :]
    vb = vp[..., idx, :].astype(jnp.float32)
    qi = jnp.arange(lp).reshape(nb, Q_BLOCK)[:, :, None]
    kj = (idx - radius)[:, None, :]
    mask = (jnp.abs(qi - kj) <= radius) & (kj >= 0) & (kj < L)
    s = jnp.einsum('...nqd,...nkd->...nqk', qb, kb, preferred_element_type=jnp.float32)
    s = jnp.where(mask, s, NEG_INF)
    m = jnp.max(s, -1)
    p = jnp.exp(s - m[..., None])
    l = jnp.sum(p, -1)
    o = jnp.einsum('...nqk,...nkd->...nqd', p, vb)
    m = m.reshape(lead + (lp,))[..., :L]
    l = l.reshape(lead + (lp,))[..., :L]
    o = o.reshape(lead + (lp, hd))[..., :L, :]
    return m, l, o


def dilated_attention(q, k, v):
    b, s, h, hd = q.shape
    ms, ls, outs = [], [], []
    for window, dil in A_CONFIGS:
        radius = window // (2 * dil)
        L = s // dil

        def to_sub(t):
            return t.reshape(b, L, dil, h, hd).transpose(0, 2, 3, 1, 4)

        m, l, o = banded_window_stats(to_sub(q), to_sub(k), to_sub(v), radius)
        ms.append(m.transpose(0, 3, 1, 2).reshape(b, s, h))
        ls.append(l.transpose(0, 3, 1, 2).reshape(b, s, h))
        outs.append(o.transpose(0, 3, 1, 2, 4).reshape(b, s, h, hd))
    m_all = jnp.stack(ms)
    l_all = jnp.stack(ls)
    o_all = jnp.stack(outs)
    m_max = jnp.max(m_all, 0)
    w = jnp.exp(m_all - m_max)
    out = jnp.sum(w[..., None] * o_all, 0) / jnp.sum(w * l_all, 0)[..., None]
    return out.astype(q.dtype)


def axial_gqa(q, k, v):
    b, s, hq, hd = q.shape
    hkv = k.shape[2]
    g = hq // hkv
    nq = s // Q_BLOCK
    qb = q.reshape(b, nq, Q_BLOCK, hkv, g, hd).transpose(1, 0, 2, 3, 4, 5)

    def block(qblk):
        sc = jnp.einsum('bqhgd,bkhd->bhgqk', qblk, k, preferred_element_type=jnp.float32)
        p = jax.nn.softmax(sc, -1)
        return jnp.einsum('bhgqk,bkhd->bqhgd', p.astype(v.dtype), v)

    o = lax.map(block, qb)
    return o.transpose(1, 0, 2, 3, 4, 5).reshape(b, s, hq * hd)


def neighbourhood_attention(q, k, v, rpb):
    b, s, h, hd = q.shape
    rows = s // GRID_W
    kr = min(C_ROWS_MAX, rows)

    def grid(t):
        return t.reshape(b, rows, GRID_W, h, hd).transpose(0, 3, 1, 2, 4)

    qg, kg, vg = grid(q), grid(k), grid(v)
    r = jnp.arange(rows)
    r0 = jnp.clip(r - kr // 2, 0, rows - kr)
    row_idx = r0[:, None] + jnp.arange(kr)[None, :]
    kn = kg[:, :, row_idx]
    vn = vg[:, :, row_idx]
    c = jnp.arange(GRID_W)
    c0 = jnp.clip(c - C_COLS // 2, 0, GRID_W - C_COLS)
    col_mask = (c[None, :] >= c0[:, None]) & (c[None, :] < c0[:, None] + C_COLS)
    dr = row_idx - r[:, None] + (C_ROWS_MAX - 1)
    dc = jnp.clip(c[None, :] - c[:, None] + (C_COLS - 1), 0, 2 * C_COLS - 2)
    bias = rpb[:, dr[:, None, :, None], dc[None, :, None, :]]
    sc = jnp.einsum('bhrqd,bhrikd->bhrqik', qg, kn, preferred_element_type=jnp.float32)
    sc = sc + bias.astype(jnp.float32)
    sc = jnp.where(col_mask[:, None, :], sc, NEG_INF)
    p = jax.nn.softmax(sc.reshape(b, h, rows, GRID_W, kr * GRID_W), -1)
    p = p.reshape(b, h, rows, GRID_W, kr, GRID_W)
    o = jnp.einsum('bhrqik,bhrikd->bhrqd', p.astype(v.dtype), vn)
    return o.transpose(0, 2, 3, 1, 4).reshape(b, s, h * hd)


def setup_inputs(seed: int = 0) -> dict:
    key = jax.random.key(seed)
    ks = jax.random.split(key, 16)
    f32 = jnp.float32

    def normal(k, shape, scale):
        return jax.random.normal(k, shape, f32) * scale

    x = normal(ks[0], (BATCH, SEQ, D_MODEL), 1.0)
    w_in = normal(ks[1], (DEPTH, D_MODEL, IN_COLS), D_MODEL ** -0.5)
    b_gate = normal(ks[2], (DEPTH, N_BRANCHES * D_MODEL), 0.02)
    q_norm_b = 1.0 + normal(ks[3], (DEPTH, HEAD_DIM), 0.02)
    k_norm_b = 1.0 + normal(ks[4], (DEPTH, HEAD_DIM), 0.02)
    rpb_c = normal(ks[5], (DEPTH, C_HEADS, 2 * C_ROWS_MAX - 1, 2 * C_COLS - 1), 0.1)
    w_branch_a = normal(ks[6], (DEPTH, A_WIDTH, D_MODEL), A_WIDTH ** -0.5 * DEEPNORM_BETA)
    w_branch_b = normal(ks[7], (DEPTH, B_Q_WIDTH, D_MODEL), B_Q_WIDTH ** -0.5 * DEEPNORM_BETA)
    w_branch_c = normal(ks[8], (DEPTH, C_WIDTH, D_MODEL), C_WIDTH ** -0.5 * DEEPNORM_BETA)
    w_out = normal(ks[9], (DEPTH, D_MODEL, D_MODEL), D_MODEL ** -0.5 * DEEPNORM_BETA)
    ln1_g = 1.0 + normal(ks[10], (DEPTH, D_MODEL), 0.02)
    ln1_b = normal(ks[11], (DEPTH, D_MODEL), 0.02)
    w_up = normal(ks[12], (DEPTH, D_MODEL, D_FF), D_MODEL ** -0.5)
    w_down = normal(ks[13], (DEPTH, D_FF, D_MODEL), D_FF ** -0.5 * DEEPNORM_BETA)
    ln2_g = 1.0 + normal(ks[14], (DEPTH, D_MODEL), 0.02)
    ln2_b = normal(ks[15], (DEPTH, D_MODEL), 0.02)
    return {"x": x, "w_in": w_in, "b_gate": b_gate, "q_norm_b": q_norm_b,
            "k_norm_b": k_norm_b, "rpb_c": rpb_c, "w_branch_a": w_branch_a,
            "w_branch_b": w_branch_b, "w_branch_c": w_branch_c, "w_out": w_out,
            "ln1_g": ln1_g, "ln1_b": ln1_b, "w_up": w_up, "w_down": w_down,
            "ln2_g": ln2_g, "ln2_b": ln2_b}


def reference(x, w_in, b_gate, q_norm_b, k_norm_b, rpb_c, w_branch_a, w_branch_b,
              w_branch_c, w_out, ln1_g, ln1_b, w_up, w_down, ln2_g, ln2_b):
    b, s, _ = x.shape
    pos = jnp.arange(s)
    row = pos // GRID_W
    col = pos % GRID_W
    scale = HEAD_DIM ** -0.5
    splits = _split_points()

    def heads(t, n):
        return t.reshape(b, s, n, HEAD_DIM)

    for layer in range(DEPTH):
        h = x @ w_in[layer]
        qa, ka, va, qb, kb, vb, qc, kc, vc, gate_logits = jnp.split(h, splits, axis=-1)

        qa = partial_rotary(heads(qa, A_HEADS), pos) * scale
        ka = partial_rotary(heads(ka, A_HEADS), pos)
        oa = dilated_attention(qa, ka, heads(va, A_HEADS)).reshape(b, s, A_WIDTH)

        qb = axial_rotary(rms_norm(heads(qb, B_Q_HEADS), q_norm_b[layer]), row, col) * scale
        kb = axial_rotary(rms_norm(heads(kb, B_KV_HEADS), k_norm_b[layer]), row, col)
        ob = axial_gqa(qb, kb, heads(vb, B_KV_HEADS))

        oc = neighbourhood_attention(heads(qc, C_HEADS) * scale, heads(kc, C_HEADS),
                                     heads(vc, C_HEADS), rpb_c[layer])

        g = jax.nn.sigmoid((gate_logits + b_gate[layer]).astype(jnp.float32)).astype(x.dtype)
        g = g.reshape(b, s, N_BRANCHES, D_MODEL)
        merged = (g[:, :, 0] * (oa @ w_branch_a[layer])
                  + g[:, :, 1] * (ob @ w_branch_b[layer])
                  + g[:, :, 2] * (oc @ w_branch_c[layer]))
        mix = merged @ w_out[layer]
        x = layer_norm(DEEPNORM_ALPHA * x + mix, ln1_g[layer], ln1_b[layer])

        ff = jnp.square(jax.nn.relu(x @ w_up[layer])) @ w_down[layer]
        x = layer_norm(DEEPNORM_ALPHA * x + ff, ln2_g[layer], ln2_b[layer])
    return x
```

```python
import functools
import math

import numpy as np
import jax
import jax.numpy as jnp
from jax import lax
from jax.experimental import pallas as pl
from jax.experimental.pallas import tpu as pltpu

HEAD_DIM = 64
A_HEADS = 4
A_CONFIGS = ((128, 1), (512, 4), (2048, 16))
A_ROPE_DIMS = HEAD_DIM // 4
A_ROPE_THETA = 500000.0
B_Q_HEADS = 8
B_KV_HEADS = 2
B_AXIAL_THETA = 10000.0
C_HEADS = 4
C_ROWS_MAX = 8
C_COLS = 16
GRID_W = 64
Q_BLOCK = 128
LN_EPS = 1e-5
RMS_EPS = 1e-6
NEG_INF = -1e30
LOG2E = math.log2(math.e)

A_WIDTH = A_HEADS * HEAD_DIM
B_Q_WIDTH = B_Q_HEADS * HEAD_DIM
B_KV_WIDTH = B_KV_HEADS * HEAD_DIM
C_WIDTH = C_HEADS * HEAD_DIM
QKV_COLS = 3 * A_WIDTH + B_Q_WIDTH + 2 * B_KV_WIDTH + 3 * C_WIDTH

VMEM_LIMIT_BYTES = 56 * 1024 * 1024
ROW_TILE = 512
B_TILE = 256
C_TILE_ROWS = 4

_NT = (((1,), (1,)), ((), ()))
_TN = (((0,), (0,)), ((), ()))


def _params(*sem):
    return pltpu.CompilerParams(dimension_semantics=sem, vmem_limit_bytes=VMEM_LIMIT_BYTES)


def _bf16(x):
    return x.astype(jnp.bfloat16)


def _dot(a, b):
    return jnp.dot(a, b, preferred_element_type=jnp.float32)


def _layer_norm(y, g, b):
    mu = jnp.mean(y, axis=-1, keepdims=True)
    d = y - mu
    var = jnp.mean(d * d, axis=-1, keepdims=True)
    return d * lax.rsqrt(var + LN_EPS) * g + b


def _rotate(x, c, sa, sb, half):
    w = x.shape[-1]
    return x * c + pltpu.roll(x, w - half, 1) * sa + pltpu.roll(x, half, 1) * sb


def _head_mean_sq(h, ones_blk):
    sq = h * h
    hi = _bf16(sq)
    lo = _bf16(sq - hi.astype(jnp.float32))
    return (_dot(hi, ones_blk) + _dot(lo, ones_blk)) * (1.0 / HEAD_DIM)


def _in_proj_kernel(x_ref, w_ref, ca_ref, saa_ref, sba_ref, cb_ref, sab_ref, sbb_ref,
                    qg_ref, kg_ref, ones_ref,
                    qa_ref, ka_ref, va_ref, qbt_ref, kb_ref, vbt_ref, qc_ref, kc_ref, vc_ref):
    xb = _bf16(x_ref[...])
    scale = HEAD_DIM ** -0.5

    ha = _dot(xb, w_ref[:, 0:3 * A_WIDTH])
    ca, saa, sba = ca_ref[...], saa_ref[...], sba_ref[...]
    half_a = A_ROPE_DIMS // 2
    qa = _rotate(ha[:, 0:A_WIDTH], ca, saa, sba, half_a) * scale
    ka = _rotate(ha[:, A_WIDTH:2 * A_WIDTH], ca, saa, sba, half_a)
    qa_ref[...] = _bf16(qa)
    ka_ref[...] = _bf16(ka)
    va_ref[...] = _bf16(ha[:, 2 * A_WIDTH:3 * A_WIDTH])

    off = 3 * A_WIDTH
    hb = _dot(xb, w_ref[:, off:off + B_Q_WIDTH + 2 * B_KV_WIDTH])
    ones_q = ones_ref[...]
    ones_k = ones_ref[0:B_KV_WIDTH, 0:B_KV_WIDTH]
    cb, sab, sbb = cb_ref[...], sab_ref[...], sbb_ref[...]
    rep = B_Q_WIDTH // B_KV_WIDTH
    cb_q = jnp.concatenate([cb] * rep, axis=1)
    sab_q = jnp.concatenate([sab] * rep, axis=1)
    sbb_q = jnp.concatenate([sbb] * rep, axis=1)
    half_b = HEAD_DIM // 4
    qb = hb[:, 0:B_Q_WIDTH]
    qb = qb * lax.rsqrt(_head_mean_sq(qb, ones_q) + RMS_EPS) * qg_ref[...]
    qb = _rotate(qb, cb_q, sab_q, sbb_q, half_b) * (scale * LOG2E)
    kb = hb[:, B_Q_WIDTH:B_Q_WIDTH + B_KV_WIDTH]
    kb = kb * lax.rsqrt(_head_mean_sq(kb, ones_k) + RMS_EPS) * kg_ref[...]
    kb = _rotate(kb, cb, sab, sbb, half_b)
    vb = hb[:, B_Q_WIDTH + B_KV_WIDTH:]
    qbt_ref[0] = _bf16(qb.T)
    kb_ref[...] = _bf16(kb)
    for c in range(vbt_ref.shape[1]):
        vbt_ref[0, c] = _bf16(vb[c * B_TILE:(c + 1) * B_TILE, :].T)

    off += B_Q_WIDTH + 2 * B_KV_WIDTH
    hc = _dot(xb, w_ref[:, off:off + 3 * C_WIDTH])
    qc_ref[...] = _bf16(hc[:, 0:C_WIDTH] * (scale * LOG2E))
    kc_ref[...] = _bf16(hc[:, C_WIDTH:2 * C_WIDTH])
    vc_ref[...] = _bf16(hc[:, 2 * C_WIDTH:3 * C_WIDTH])


def _in_proj(x2, w_qkv, tabs_a, tabs_b, qg, kg, ones_blk, batch, seq):
    n, d = x2.shape
    tm = ROW_TILE
    nt = seq // tm
    row = lambda i: (i, 0)
    pos = lambda i: (i % nt, 0)
    const = lambda i: (0, 0)
    bf = jnp.bfloat16
    out_shape = (
        jax.ShapeDtypeStruct((n, A_WIDTH), bf), jax.ShapeDtypeStruct((n, A_WIDTH), bf),
        jax.ShapeDtypeStruct((n, A_WIDTH), bf),
        jax.ShapeDtypeStruct((batch, B_Q_WIDTH, seq), bf),
        jax.ShapeDtypeStruct((n, B_KV_WIDTH), bf),
        jax.ShapeDtypeStruct((batch, seq // B_TILE, B_KV_WIDTH, B_TILE), bf),
        jax.ShapeDtypeStruct((n, C_WIDTH), bf), jax.ShapeDtypeStruct((n, C_WIDTH), bf),
        jax.ShapeDtypeStruct((n, C_WIDTH), bf),
    )
    out_specs = (
        pl.BlockSpec((tm, A_WIDTH), row), pl.BlockSpec((tm, A_WIDTH), row),
        pl.BlockSpec((tm, A_WIDTH), row),
        pl.BlockSpec((1, B_Q_WIDTH, tm), lambda i: (i // nt, 0, i % nt)),
        pl.BlockSpec((tm, B_KV_WIDTH), row),
        pl.BlockSpec((1, tm // B_TILE, B_KV_WIDTH, B_TILE), lambda i: (i // nt, i % nt, 0, 0)),
        pl.BlockSpec((tm, C_WIDTH), row), pl.BlockSpec((tm, C_WIDTH), row),
        pl.BlockSpec((tm, C_WIDTH), row),
    )
    in_specs = [
        pl.BlockSpec((tm, d), row),
        pl.BlockSpec(w_qkv.shape, const),
        pl.BlockSpec((tm, A_WIDTH), pos), pl.BlockSpec((tm, A_WIDTH), pos),
        pl.BlockSpec((tm, A_WIDTH), pos),
        pl.BlockSpec((tm, B_KV_WIDTH), pos), pl.BlockSpec((tm, B_KV_WIDTH), pos),
        pl.BlockSpec((tm, B_KV_WIDTH), pos),
        pl.BlockSpec(qg.shape, const), pl.BlockSpec(kg.shape, const),
        pl.BlockSpec(ones_blk.shape, const),
    ]
    return pl.pallas_call(
        _in_proj_kernel, grid=(n // tm,), in_specs=in_specs, out_specs=out_specs,
        out_shape=out_shape, compiler_params=_params("parallel"), name="in_proj",
    )(x2, w_qkv, *tabs_a, *tabs_b, qg, kg, ones_blk)


def _attn_a_kernel(*refs, first, last, sub_len, radius):
    if first:
        q_ref, k_ref, v_ref = refs[:3]
        outs = refs[3:]
    else:
        q_ref, k_ref, v_ref, m_in, l_in, o_in = refs[:6]
        outs = refs[6:]
    i = pl.program_id(2)
    span = Q_BLOCK + 2 * radius
    start = pl.multiple_of(jnp.clip(i * Q_BLOCK - radius, 0, sub_len - span), radius)
    kw = k_ref[0, pl.ds(start, span), :]
    vw = v_ref[0, pl.ds(start, span), :]
    qpos = i * Q_BLOCK + lax.broadcasted_iota(jnp.int32, (Q_BLOCK, span), 0)
    kpos = start + lax.broadcasted_iota(jnp.int32, (Q_BLOCK, span), 1)
    mask = jnp.abs(qpos - kpos) <= radius
    q = q_ref[0]
    for h in range(A_HEADS):
        sl = slice(h * HEAD_DIM, (h + 1) * HEAD_DIM)
        s = lax.dot_general(q[:, sl], kw[:, sl], _NT, preferred_element_type=jnp.float32)
        s = jnp.where(mask, s, NEG_INF)
        m_cur = jnp.max(s, axis=-1, keepdims=True)
        if first:
            m_new = m_cur
        else:
            m_prev = m_in[0, :, sl][:, 0:1]
            m_new = jnp.maximum(m_prev, m_cur)
        p = jnp.exp(s - m_new)
        l_new = jnp.sum(p, axis=-1, keepdims=True)
        o_new = _dot(_bf16(p), vw[:, sl])
        if not first:
            alpha = jnp.exp(m_prev - m_new)
            l_new = l_new + alpha * l_in[0, :, sl][:, 0:1]
            o_new = o_new + alpha * o_in[0, :, sl]
        if last:
            outs[0][0, :, sl] = _bf16(o_new / l_new)
        else:
            outs[0][0, :, sl] = jnp.broadcast_to(m_new, (Q_BLOCK, HEAD_DIM))
            outs[1][0, :, sl] = jnp.broadcast_to(l_new, (Q_BLOCK, HEAD_DIM))
            outs[2][0, :, sl] = o_new


def _attn_a(qa, ka, va, batch, seq):
    state = None
    n_cfg = len(A_CONFIGS)
    for ci, (window, dil) in enumerate(A_CONFIGS):
        radius = window // (2 * dil)
        sub_len = seq // dil
        first, last = ci == 0, ci == n_cfg - 1
        view = lambda t: t.reshape(batch, sub_len, dil * A_WIDTH)
        blk = pl.BlockSpec((1, Q_BLOCK, A_WIDTH), lambda b, r, i: (b, i, r))
        full = pl.BlockSpec((1, sub_len, A_WIDTH), lambda b, r, i: (b, 0, r))
        args = [view(qa), view(ka), view(va)]
        in_specs = [blk, full, full]
        if not first:
            args += [view(s) for s in state]
            in_specs += [blk, blk, blk]
        if last:
            out_shape = (jax.ShapeDtypeStruct((batch, sub_len, dil * A_WIDTH), jnp.bfloat16),)
        else:
            out_shape = tuple(jax.ShapeDtypeStruct((batch, sub_len, dil * A_WIDTH), jnp.float32)
                              for _ in range(3))
        outs = pl.pallas_call(
            functools.partial(_attn_a_kernel, first=first, last=last, sub_len=sub_len, radius=radius),
            grid=(batch, dil, sub_len // Q_BLOCK), in_specs=in_specs,
            out_specs=tuple(blk for _ in out_shape), out_shape=out_shape,
            compiler_params=_params("parallel", "parallel", "arbitrary"), name=f"attn_a_d{dil}",
        )(*args)
        state = [o.reshape(batch * seq, A_WIDTH) for o in outs]
    return state[0]


def _attn_b_kernel(qt_ref, k_ref, vt_ref, ot_ref, wq_sc, m_sc, l_sc, acc_sc):
    tq = qt_ref.shape[2]
    n_kt = vt_ref.shape[1]
    group = B_Q_HEADS // B_KV_HEADS
    wq_sc[...] = jnp.zeros(wq_sc.shape, wq_sc.dtype)
    for h in range(B_Q_HEADS):
        g = h // group
        wq_sc[h, g * HEAD_DIM:(g + 1) * HEAD_DIM, :] = qt_ref[0, h * HEAD_DIM:(h + 1) * HEAD_DIM, :]
    m_sc[...] = jnp.full(m_sc.shape, NEG_INF, jnp.float32)
    l_sc[...] = jnp.zeros(l_sc.shape, jnp.float32)
    acc_sc[...] = jnp.zeros(acc_sc.shape, jnp.float32)

    def body(j, carry):
        kt = k_ref[0, pl.ds(pl.multiple_of(j * B_TILE, B_TILE), B_TILE), :]
        for h in range(B_Q_HEADS):
            g = h // group
            st = _dot(kt, wq_sc[h])
            m_prev = m_sc[h]
            m_new = jnp.maximum(m_prev, jnp.max(st, axis=0, keepdims=True))
            alpha = jnp.exp2(m_prev - m_new)
            p = jnp.exp2(st - m_new)
            vt = vt_ref[0, j, g * HEAD_DIM:(g + 1) * HEAD_DIM, :]
            l_sc[h] = alpha * l_sc[h] + jnp.sum(p, axis=0, keepdims=True)
            acc_sc[h] = alpha * acc_sc[h] + _dot(vt, _bf16(p))
            m_sc[h] = m_new
        return carry

    lax.fori_loop(0, n_kt, body, 0)
    for h in range(B_Q_HEADS):
        ot_ref[0, h * HEAD_DIM:(h + 1) * HEAD_DIM, :] = _bf16(acc_sc[h] / l_sc[h])


def _attn_b(qbt, kb, vbt, batch, seq):
    tq = B_TILE
    kb3 = kb.reshape(batch, seq, B_KV_WIDTH)
    return pl.pallas_call(
        _attn_b_kernel, grid=(batch, seq // tq),
        in_specs=[
            pl.BlockSpec((1, B_Q_WIDTH, tq), lambda b, i: (b, 0, i)),
            pl.BlockSpec((1, seq, B_KV_WIDTH), lambda b, i: (b, 0, 0)),
            pl.BlockSpec((1, seq // B_TILE, B_KV_WIDTH, B_TILE), lambda b, i: (b, 0, 0, 0)),
        ],
        out_specs=pl.BlockSpec((1, B_Q_WIDTH, tq), lambda b, i: (b, 0, i)),
        out_shape=jax.ShapeDtypeStruct((batch, B_Q_WIDTH, seq), jnp.bfloat16),
        scratch_shapes=[
            pltpu.VMEM((B_Q_HEADS, B_KV_WIDTH, tq), jnp.bfloat16),
            pltpu.VMEM((B_Q_HEADS, 1, tq), jnp.float32),
            pltpu.VMEM((B_Q_HEADS, 1, tq), jnp.float32),
            pltpu.VMEM((B_Q_HEADS, HEAD_DIM, tq), jnp.float32),
        ],
        compiler_params=_params("parallel", "parallel"), name="attn_b",
    )(qbt, kb3, vbt)


def _c_window_rows():
    return C_TILE_ROWS + C_ROWS_MAX - 1


def _attn_c_kernel(q_ref, k_ref, v_ref, bias_ref, o_ref, *, rows):
    t = pl.program_id(1)
    win = _c_window_rows() * GRID_W
    ws = jnp.clip(t * C_TILE_ROWS - C_ROWS_MAX // 2, 0, rows - _c_window_rows())
    start = pl.multiple_of(ws * GRID_W, GRID_W)
    kw = k_ref[0, pl.ds(start, win), :]
    vw = v_ref[0, pl.ds(start, win), :]
    q = q_ref[0]
    lane_head = lax.broadcasted_iota(jnp.int32, q.shape, 1) // HEAD_DIM
    out = jnp.zeros(q.shape, jnp.float32)
    for h in range(C_HEADS):
        qh = jnp.where(lane_head == h, q, jnp.zeros_like(q))
        s = lax.dot_general(qh, kw, _NT, preferred_element_type=jnp.float32) + bias_ref[0, h]
        m = jnp.max(s, axis=-1, keepdims=True)
        p = jnp.exp2(s - m)
        l = jnp.sum(p, axis=-1, keepdims=True)
        pv = _dot(_bf16(p), vw)
        out = jnp.where(lane_head == h, pv / l, out)
    o_ref[0] = _bf16(out)


def _c_bias_tables(rpb, rows):
    n_tiles = rows // C_TILE_ROWS
    wr = _c_window_rows()
    kr = min(C_ROWS_MAX, rows)
    tabs = []
    for t in (0, 1, n_tiles - 1):
        ws = int(np.clip(t * C_TILE_ROWS - C_ROWS_MAX // 2, 0, rows - wr))
        r = t * C_TILE_ROWS + np.arange(C_TILE_ROWS)
        r0 = np.clip(r - kr // 2, 0, rows - kr)
        krow = ws + np.arange(wr)
        row_ok = (krow[None, :] >= r0[:, None]) & (krow[None, :] < r0[:, None] + kr)
        dr = np.clip(krow[None, :] - r[:, None] + (C_ROWS_MAX - 1), 0, 2 * C_ROWS_MAX - 2)
        c = np.arange(GRID_W)
        c0 = np.clip(c - C_COLS // 2, 0, GRID_W - C_COLS)
        col_ok = (c[None, :] >= c0[:, None]) & (c[None, :] < c0[:, None] + C_COLS)
        dc = np.clip(c[None, :] - c[:, None] + (C_COLS - 1), 0, 2 * C_COLS - 2)
        bias = rpb[:, dr[:, None, :, None], dc[None, :, None, :]]
        ok = row_ok[:, None, :, None] & col_ok[None, :, None, :]
        bias = jnp.where(ok[None], bias.astype(jnp.float32) * LOG2E, NEG_INF)
        tabs.append(bias.reshape(rpb.shape[0], C_TILE_ROWS * GRID_W, wr * GRID_W))
    return jnp.stack(tabs)


def _attn_c(qc, kc, vc, bias_tabs, batch, seq):
    rows = seq // GRID_W
    n_tiles = rows // C_TILE_ROWS
    tq = C_TILE_ROWS * GRID_W
    view = lambda t: t.reshape(batch, seq, C_WIDTH)
    cls = lambda b, t: (jnp.where(t == 0, 0, jnp.where(t == n_tiles - 1, 2, 1)), 0, 0, 0)
    out = pl.pallas_call(
        functools.partial(_attn_c_kernel, rows=rows), grid=(batch, n_tiles),
        in_specs=[
            pl.BlockSpec((1, tq, C_WIDTH), lambda b, t: (b, t, 0)),
            pl.BlockSpec((1, seq, C_WIDTH), lambda b, t: (b, 0, 0)),
            pl.BlockSpec((1, seq, C_WIDTH), lambda b, t: (b, 0, 0)),
            pl.BlockSpec((1,) + bias_tabs.shape[1:], cls),
        ],
        out_specs=pl.BlockSpec((1, tq, C_WIDTH), lambda b, t: (b, t, 0)),
        out_shape=jax.ShapeDtypeStruct((batch, seq, C_WIDTH), jnp.bfloat16),
        compiler_params=_params("parallel", "arbitrary"), name="attn_c",
    )(view(qc), view(kc), view(vc), bias_tabs)
    return out.reshape(batch * seq, C_WIDTH)


def _mix_kernel(x_ref, oa_ref, obt_ref, oc_ref, wg_ref, bg_ref, wa_ref, wb_ref, wc_ref, wo_ref,
                g_ref, b_ref, y_ref, *, alpha):
    x = x_ref[...]
    xb = _bf16(x)
    d = x.shape[1]
    projs = (
        _dot(oa_ref[...], wa_ref[...]),
        lax.dot_general(obt_ref[0], wb_ref[...], _TN, preferred_element_type=jnp.float32),
        _dot(oc_ref[...], wc_ref[...]),
    )
    merged = None
    for br, proj in enumerate(projs):
        logits = _dot(xb, wg_ref[:, br * d:(br + 1) * d]) + bg_ref[:, br * d:(br + 1) * d]
        term = jax.nn.sigmoid(logits) * proj
        merged = term if merged is None else merged + term
    mix = _dot(_bf16(merged), wo_ref[...])
    y_ref[...] = _layer_norm(alpha * x + mix, g_ref[...], b_ref[...])


def _mix(x2, oa, obt, oc, wg, bg, wa, wb, wc, wo, g, b, alpha, seq):
    n, d = x2.shape
    tm = ROW_TILE
    nt = seq // tm
    row = lambda i: (i, 0)
    const = lambda i: (0, 0)
    full = lambda a: pl.BlockSpec(a.shape, const)
    return pl.pallas_call(
        functools.partial(_mix_kernel, alpha=alpha), grid=(n // tm,),
        in_specs=[
            pl.BlockSpec((tm, d), row), pl.BlockSpec((tm, A_WIDTH), row),
            pl.BlockSpec((1, B_Q_WIDTH, tm), lambda i: (i // nt, 0, i % nt)),
            pl.BlockSpec((tm, C_WIDTH), row),
            full(wg), full(bg), full(wa), full(wb), full(wc), full(wo), full(g), full(b),
        ],
        out_specs=pl.BlockSpec((tm, d), row),
        out_shape=jax.ShapeDtypeStruct((n, d), jnp.float32),
        compiler_params=_params("parallel"), name="mix",
    )(x2, oa, obt, oc, wg, bg, wa, wb, wc, wo, g, b)


def _mlp_kernel(x_ref, wu_ref, wd_ref, g_ref, b_ref, y_ref, *, alpha, chunk):
    x = x_ref[...]
    xb = _bf16(x)
    ff = None
    for c in range(wu_ref.shape[1] // chunk):
        hid = jnp.square(jnp.maximum(_dot(xb, wu_ref[:, c * chunk:(c + 1) * chunk]), 0.0))
        part = _dot(_bf16(hid), wd_ref[c * chunk:(c + 1) * chunk, :])
        ff = part if ff is None else ff + part
    y_ref[...] = _layer_norm(alpha * x + ff, g_ref[...], b_ref[...])


def _mlp(x2, wu, wd, g, b, alpha):
    n, d = x2.shape
    tm = ROW_TILE
    row = lambda i: (i, 0)
    const = lambda i: (0, 0)
    single = pl.Buffered(1)
    return pl.pallas_call(
        functools.partial(_mlp_kernel, alpha=alpha, chunk=1024), grid=(n // tm,),
        in_specs=[
            pl.BlockSpec((tm, d), row),
            pl.BlockSpec(wu.shape, const, pipeline_mode=single),
            pl.BlockSpec(wd.shape, const, pipeline_mode=single),
            pl.BlockSpec(g.shape, const), pl.BlockSpec(b.shape, const),
        ],
        out_specs=pl.BlockSpec((tm, d), row),
        out_shape=jax.ShapeDtypeStruct((n, d), jnp.float32),
        compiler_params=_params("parallel"), name="mlp",
    )(x2, wu, wd, g, b)


def _rotary_tables(pos_list, theta_list, half, width_per_head, reps):
    s = pos_list[0].shape[0]
    c = jnp.ones((s, width_per_head), jnp.float32)
    sa = jnp.zeros((s, width_per_head), jnp.float32)
    sb = jnp.zeros((s, width_per_head), jnp.float32)
    for i, (pos, theta) in enumerate(zip(pos_list, theta_list)):
        inv = theta ** (-jnp.arange(half, dtype=jnp.float32) / half)
        ang = pos.astype(jnp.float32)[:, None] * inv[None, :]
        cos, sin = jnp.cos(ang), jnp.sin(ang)
        o = 2 * half * i
        c = c.at[:, o:o + half].set(cos).at[:, o + half:o + 2 * half].set(cos)
        sa = sa.at[:, o:o + half].set(-sin)
        sb = sb.at[:, o + half:o + 2 * half].set(sin)
    tile = lambda t: jnp.tile(t, (1, reps))
    return tile(c), tile(sa), tile(sb)


def kernel(x, w_in, b_gate, q_norm_b, k_norm_b, rpb_c, w_branch_a, w_branch_b, w_branch_c, w_out,
           ln1_g, ln1_b, w_up, w_down, ln2_g, ln2_b):
    batch, seq, d = x.shape
    depth = w_in.shape[0]
    alpha = (2 * depth) ** 0.25
    rows = seq // GRID_W
    assert seq % (ROW_TILE) == 0 and rows % C_TILE_ROWS == 0 and rows >= 2 * _c_window_rows()
    assert all((seq // dil) >= Q_BLOCK + window // dil for window, dil in A_CONFIGS)

    pos = jnp.arange(seq)
    tabs_a = _rotary_tables([pos], [A_ROPE_THETA], A_ROPE_DIMS // 2, HEAD_DIM, A_HEADS)
    tabs_b = _rotary_tables([pos // GRID_W, pos % GRID_W], [B_AXIAL_THETA, B_AXIAL_THETA],
                            HEAD_DIM // 4, HEAD_DIM, B_KV_HEADS)
    head_of = np.arange(B_Q_WIDTH) // HEAD_DIM
    ones_blk = jnp.asarray(head_of[:, None] == head_of[None, :], jnp.bfloat16)

    bf = jnp.bfloat16
    x2 = x.reshape(batch * seq, d)
    for layer in range(depth):
        w_qkv = w_in[layer, :, :QKV_COLS].astype(bf)
        w_gate = w_in[layer, :, QKV_COLS:].astype(bf)
        qg = jnp.tile(q_norm_b[layer], B_Q_HEADS)[None, :]
        kg = jnp.tile(k_norm_b[layer], B_KV_HEADS)[None, :]
        qa, ka, va, qbt, kb, vbt, qc, kc, vc = _in_proj(
            x2, w_qkv, tabs_a, tabs_b, qg, kg, ones_blk, batch, seq)
        oa = _attn_a(qa, ka, va, batch, seq)
        obt = _attn_b(qbt, kb, vbt, batch, seq)
        oc = _attn_c(qc, kc, vc, _c_bias_tables(rpb_c[layer], rows), batch, seq)
        x2 = _mix(x2, oa, obt, oc, w_gate, b_gate[layer][None, :],
                  w_branch_a[layer].astype(bf), w_branch_b[layer].astype(bf),
                  w_branch_c[layer].astype(bf), w_out[layer].astype(bf),
                  ln1_g[layer][None, :], ln1_b[layer][None, :], alpha, seq)
        x2 = _mlp(x2, w_up[layer].astype(bf), w_down[layer].astype(bf),
                  ln2_g[layer][None, :], ln2_b[layer][None, :], alpha)
    return x2.reshape(batch, seq, d)
```

```python
import functools
import math

import numpy as np
import jax
import jax.numpy as jnp
from jax import lax
from jax.experimental import pallas as pl
from jax.experimental.pallas import tpu as pltpu

HEAD_DIM = 64
A_HEADS = 4
A_CONFIGS = ((128, 1), (512, 4), (2048, 16))
A_ROPE_DIMS = HEAD_DIM // 4
A_ROPE_THETA = 500000.0
B_Q_HEADS = 8
B_KV_HEADS = 2
B_AXIAL_THETA = 10000.0
C_HEADS = 4
C_ROWS_MAX = 8
C_COLS = 16
GRID_W = 64
Q_BLOCK = 128
LN_EPS = 1e-5
RMS_EPS = 1e-6
NEG_INF = -1e30
LOG2E = math.log2(math.e)

A_WIDTH = A_HEADS * HEAD_DIM
B_Q_WIDTH = B_Q_HEADS * HEAD_DIM
B_KV_WIDTH = B_KV_HEADS * HEAD_DIM
C_WIDTH = C_HEADS * HEAD_DIM
QKV_COLS = 3 * A_WIDTH + B_Q_WIDTH + 2 * B_KV_WIDTH + 3 * C_WIDTH

VMEM_LIMIT_BYTES = 56 * 1024 * 1024
ROW_TILE = 512
B_TILE = 256
B_VT_ROWS = HEAD_DIM + 16
C_TILE_ROWS = 4

_NT = (((1,), (1,)), ((), ()))
_TN = (((0,), (0,)), ((), ()))


def _params(*sem):
    return pltpu.CompilerParams(dimension_semantics=sem, vmem_limit_bytes=VMEM_LIMIT_BYTES)


def _bf16(x):
    return x.astype(jnp.bfloat16)


def _dot(a, b):
    return jnp.dot(a, b, preferred_element_type=jnp.float32)


def _layer_norm(y, g, b):
    mu = jnp.mean(y, axis=-1, keepdims=True)
    d = y - mu
    var = jnp.mean(d * d, axis=-1, keepdims=True)
    return d * lax.rsqrt(var + LN_EPS) * g + b


def _rotate(x, c, sa, sb, half):
    w = x.shape[-1]
    return x * c + pltpu.roll(x, w - half, 1) * sa + pltpu.roll(x, half, 1) * sb


def _head_mean_sq(h, ones_blk):
    sq = h * h
    hi = _bf16(sq)
    lo = _bf16(sq - hi.astype(jnp.float32))
    return (_dot(hi, ones_blk) + _dot(lo, ones_blk)) * (1.0 / HEAD_DIM)


def _in_proj_kernel(x_ref, w_ref, ca_ref, saa_ref, sba_ref, cb_ref, sab_ref, sbb_ref,
                    qg_ref, kg_ref, ones_ref,
                    qa_ref, ka_ref, va_ref, qbt_ref, kb_ref, vbt_ref, qc_ref, kc_ref, vc_ref):
    xb = _bf16(x_ref[...])
    scale = HEAD_DIM ** -0.5

    ha = _dot(xb, w_ref[:, 0:3 * A_WIDTH])
    ca, saa, sba = ca_ref[...], saa_ref[...], sba_ref[...]
    half_a = A_ROPE_DIMS // 2
    qa = _rotate(ha[:, 0:A_WIDTH], ca, saa, sba, half_a) * scale
    ka = _rotate(ha[:, A_WIDTH:2 * A_WIDTH], ca, saa, sba, half_a)
    qa_ref[...] = _bf16(qa)
    ka_ref[...] = _bf16(ka)
    va_ref[...] = _bf16(ha[:, 2 * A_WIDTH:3 * A_WIDTH])

    off = 3 * A_WIDTH
    hb = _dot(xb, w_ref[:, off:off + B_Q_WIDTH + 2 * B_KV_WIDTH])
    ones_q = ones_ref[...]
    ones_k = ones_ref[0:B_KV_WIDTH, 0:B_KV_WIDTH]
    cb, sab, sbb = cb_ref[...], sab_ref[...], sbb_ref[...]
    rep = B_Q_WIDTH // B_KV_WIDTH
    cb_q = jnp.concatenate([cb] * rep, axis=1)
    sab_q = jnp.concatenate([sab] * rep, axis=1)
    sbb_q = jnp.concatenate([sbb] * rep, axis=1)
    half_b = HEAD_DIM // 4
    qb = hb[:, 0:B_Q_WIDTH]
    qb = qb * lax.rsqrt(_head_mean_sq(qb, ones_q) + RMS_EPS) * qg_ref[...]
    qb = _rotate(qb, cb_q, sab_q, sbb_q, half_b) * (scale * LOG2E)
    kb = hb[:, B_Q_WIDTH:B_Q_WIDTH + B_KV_WIDTH]
    kb = kb * lax.rsqrt(_head_mean_sq(kb, ones_k) + RMS_EPS) * kg_ref[...]
    kb = _rotate(kb, cb, sab, sbb, half_b)
    vb = hb[:, B_Q_WIDTH + B_KV_WIDTH:]
    qbt_ref[0] = _bf16(qb.T)
    kb_ref[...] = _bf16(kb)
    ones_rows = jnp.ones((B_VT_ROWS - HEAD_DIM, B_TILE), jnp.bfloat16)
    for c in range(vbt_ref.shape[1]):
        vt = _bf16(vb[c * B_TILE:(c + 1) * B_TILE, :].T)
        for g in range(B_KV_HEADS):
            vbt_ref[0, c, g * B_VT_ROWS:g * B_VT_ROWS + HEAD_DIM, :] = vt[g * HEAD_DIM:(g + 1) * HEAD_DIM]
            vbt_ref[0, c, g * B_VT_ROWS + HEAD_DIM:(g + 1) * B_VT_ROWS, :] = ones_rows

    off += B_Q_WIDTH + 2 * B_KV_WIDTH
    hc = _dot(xb, w_ref[:, off:off + 3 * C_WIDTH])
    qc_ref[...] = _bf16(hc[:, 0:C_WIDTH] * (scale * LOG2E))
    kc_ref[...] = _bf16(hc[:, C_WIDTH:2 * C_WIDTH])
    vc_ref[...] = _bf16(hc[:, 2 * C_WIDTH:3 * C_WIDTH])


def _in_proj(x2, w_qkv, tabs_a, tabs_b, qg, kg, ones_blk, batch, seq):
    n, d = x2.shape
    tm = ROW_TILE
    nt = seq // tm
    row = lambda i: (i, 0)
    pos = lambda i: (i % nt, 0)
    const = lambda i: (0, 0)
    bf = jnp.bfloat16
    out_shape = (
        jax.ShapeDtypeStruct((n, A_WIDTH), bf), jax.ShapeDtypeStruct((n, A_WIDTH), bf),
        jax.ShapeDtypeStruct((n, A_WIDTH), bf),
        jax.ShapeDtypeStruct((batch, B_Q_WIDTH, seq), bf),
        jax.ShapeDtypeStruct((n, B_KV_WIDTH), bf),
        jax.ShapeDtypeStruct((batch, seq // B_TILE, B_KV_HEADS * B_VT_ROWS, B_TILE), bf),
        jax.ShapeDtypeStruct((n, C_WIDTH), bf), jax.ShapeDtypeStruct((n, C_WIDTH), bf),
        jax.ShapeDtypeStruct((n, C_WIDTH), bf),
    )
    out_specs = (
        pl.BlockSpec((tm, A_WIDTH), row), pl.BlockSpec((tm, A_WIDTH), row),
        pl.BlockSpec((tm, A_WIDTH), row),
        pl.BlockSpec((1, B_Q_WIDTH, tm), lambda i: (i // nt, 0, i % nt)),
        pl.BlockSpec((tm, B_KV_WIDTH), row),
        pl.BlockSpec((1, tm // B_TILE, B_KV_HEADS * B_VT_ROWS, B_TILE),
                     lambda i: (i // nt, i % nt, 0, 0)),
        pl.BlockSpec((tm, C_WIDTH), row), pl.BlockSpec((tm, C_WIDTH), row),
        pl.BlockSpec((tm, C_WIDTH), row),
    )
    in_specs = [
        pl.BlockSpec((tm, d), row),
        pl.BlockSpec(w_qkv.shape, const),
        pl.BlockSpec((tm, A_WIDTH), pos), pl.BlockSpec((tm, A_WIDTH), pos),
        pl.BlockSpec((tm, A_WIDTH), pos),
        pl.BlockSpec((tm, B_KV_WIDTH), pos), pl.BlockSpec((tm, B_KV_WIDTH), pos),
        pl.BlockSpec((tm, B_KV_WIDTH), pos),
        pl.BlockSpec(qg.shape, const), pl.BlockSpec(kg.shape, const),
        pl.BlockSpec(ones_blk.shape, const),
    ]
    return pl.pallas_call(
        _in_proj_kernel, grid=(n // tm,), in_specs=in_specs, out_specs=out_specs,
        out_shape=out_shape, compiler_params=_params("parallel"), name="in_proj",
    )(x2, w_qkv, *tabs_a, *tabs_b, qg, kg, ones_blk)


def _attn_a_kernel(*refs, first, last, sub_len, radius):
    if first:
        q_ref, k_ref, v_ref = refs[:3]
        outs = refs[3:]
    else:
        q_ref, k_ref, v_ref, m_in, l_in, o_in = refs[:6]
        outs = refs[6:]
    i = pl.program_id(2)
    span = Q_BLOCK + 2 * radius
    start = pl.multiple_of(jnp.clip(i * Q_BLOCK - radius, 0, sub_len - span), radius)
    kw = k_ref[0, pl.ds(start, span), :]
    vw = v_ref[0, pl.ds(start, span), :]
    qpos = i * Q_BLOCK + lax.broadcasted_iota(jnp.int32, (Q_BLOCK, span), 0)
    kpos = start + lax.broadcasted_iota(jnp.int32, (Q_BLOCK, span), 1)
    mask = jnp.abs(qpos - kpos) <= radius
    q = q_ref[0]
    for h in range(A_HEADS):
        sl = slice(h * HEAD_DIM, (h + 1) * HEAD_DIM)
        s = lax.dot_general(q[:, sl], kw[:, sl], _NT, preferred_element_type=jnp.float32)
        s = jnp.where(mask, s, NEG_INF)
        m_cur = jnp.max(s, axis=-1, keepdims=True)
        if first:
            m_new = m_cur
        else:
            m_prev = m_in[0, :, sl][:, 0:1]
            m_new = jnp.maximum(m_prev, m_cur)
        p = jnp.exp(s - m_new)
        l_new = jnp.sum(p, axis=-1, keepdims=True)
        o_new = _dot(_bf16(p), vw[:, sl])
        if not first:
            alpha = jnp.exp(m_prev - m_new)
            l_new = l_new + alpha * l_in[0, :, sl][:, 0:1]
            o_new = o_new + alpha * o_in[0, :, sl]
        if last:
            outs[0][0, :, sl] = _bf16(o_new / l_new)
        else:
            outs[0][0, :, sl] = jnp.broadcast_to(m_new, (Q_BLOCK, HEAD_DIM))
            outs[1][0, :, sl] = jnp.broadcast_to(l_new, (Q_BLOCK, HEAD_DIM))
            outs[2][0, :, sl] = o_new


def _attn_a(qa, ka, va, batch, seq):
    state = None
    n_cfg = len(A_CONFIGS)
    for ci, (window, dil) in enumerate(A_CONFIGS):
        radius = window // (2 * dil)
        sub_len = seq // dil
        first, last = ci == 0, ci == n_cfg - 1
        view = lambda t: t.reshape(batch, sub_len, dil * A_WIDTH)
        blk = pl.BlockSpec((1, Q_BLOCK, A_WIDTH), lambda b, r, i: (b, i, r))
        full = pl.BlockSpec((1, sub_len, A_WIDTH), lambda b, r, i: (b, 0, r))
        args = [view(qa), view(ka), view(va)]
        in_specs = [blk, full, full]
        if not first:
            args += [view(s) for s in state]
            in_specs += [blk, blk, blk]
        if last:
            out_shape = (jax.ShapeDtypeStruct((batch, sub_len, dil * A_WIDTH), jnp.bfloat16),)
        else:
            out_shape = tuple(jax.ShapeDtypeStruct((batch, sub_len, dil * A_WIDTH), jnp.float32)
                              for _ in range(3))
        outs = pl.pallas_call(
            functools.partial(_attn_a_kernel, first=first, last=last, sub_len=sub_len, radius=radius),
            grid=(batch, dil, sub_len // Q_BLOCK), in_specs=in_specs,
            out_specs=tuple(blk for _ in out_shape), out_shape=out_shape,
            compiler_params=_params("parallel", "parallel", "arbitrary"), name=f"attn_a_d{dil}",
        )(*args)
        state = [o.reshape(batch * seq, A_WIDTH) for o in outs]
    return state[0]


def _attn_b_kernel(qt_ref, k_ref, vt_ref, ot_ref, wq_sc, m_sc, acc_sc, st0_sc):
    assert B_KV_HEADS == 2
    tq = qt_ref.shape[2]
    n_kt = vt_ref.shape[1]
    group = B_Q_HEADS // B_KV_HEADS
    wq_sc[...] = jnp.zeros(wq_sc.shape, wq_sc.dtype)
    for h in range(B_Q_HEADS):
        g, hh = divmod(h, group)
        wq_sc[g, g * HEAD_DIM:(g + 1) * HEAD_DIM, hh * tq:(hh + 1) * tq] = (
            qt_ref[0, h * HEAD_DIM:(h + 1) * HEAD_DIM, :])
    m_sc[...] = jnp.full(m_sc.shape, NEG_INF, jnp.float32)
    acc_sc[...] = jnp.zeros(acc_sc.shape, jnp.float32)

    def k_tile(j):
        return k_ref[0, pl.ds(pl.multiple_of(j * B_TILE, B_TILE), B_TILE), :]

    def softmax_pv(j, g, st):
        m_prev = m_sc[g]
        m_new = jnp.maximum(m_prev, jnp.max(st, axis=0, keepdims=True))
        alpha = jnp.exp2(m_prev - m_new)
        p = jnp.exp2(st - m_new)
        vt = vt_ref[0, j, g * B_VT_ROWS:(g + 1) * B_VT_ROWS, :]
        acc_sc[g] = alpha * acc_sc[g] + _dot(vt, _bf16(p))
        m_sc[g] = m_new

    st0_sc[...] = _dot(k_tile(0), wq_sc[0])

    def body(j, carry):
        st1 = _dot(k_tile(j), wq_sc[1])
        softmax_pv(j, 0, st0_sc[...])
        st0_sc[...] = _dot(k_tile(jnp.minimum(j + 1, n_kt - 1)), wq_sc[0])
        softmax_pv(j, 1, st1)
        return carry

    lax.fori_loop(0, n_kt, body, 0)
    for h in range(B_Q_HEADS):
        g, hh = divmod(h, group)
        sl = slice(hh * tq, (hh + 1) * tq)
        ot_ref[0, h * HEAD_DIM:(h + 1) * HEAD_DIM, :] = _bf16(
            acc_sc[g, 0:HEAD_DIM, sl] / acc_sc[g, HEAD_DIM:HEAD_DIM + 1, sl])


def _attn_b(qbt, kb, vbt, batch, seq):
    tq = B_TILE
    group = B_Q_HEADS // B_KV_HEADS
    kb3 = kb.reshape(batch, seq, B_KV_WIDTH)
    return pl.pallas_call(
        _attn_b_kernel, grid=(batch, seq // tq),
        in_specs=[
            pl.BlockSpec((1, B_Q_WIDTH, tq), lambda b, i: (b, 0, i)),
            pl.BlockSpec((1, seq, B_KV_WIDTH), lambda b, i: (b, 0, 0)),
            pl.BlockSpec((1, seq // B_TILE, B_KV_HEADS * B_VT_ROWS, B_TILE), lambda b, i: (b, 0, 0, 0)),
        ],
        out_specs=pl.BlockSpec((1, B_Q_WIDTH, tq), lambda b, i: (b, 0, i)),
        out_shape=jax.ShapeDtypeStruct((batch, B_Q_WIDTH, seq), jnp.bfloat16),
        scratch_shapes=[
            pltpu.VMEM((B_KV_HEADS, B_KV_WIDTH, group * tq), jnp.bfloat16),
            pltpu.VMEM((B_KV_HEADS, 1, group * tq), jnp.float32),
            pltpu.VMEM((B_KV_HEADS, B_VT_ROWS, group * tq), jnp.float32),
            pltpu.VMEM((B_TILE, group * tq), jnp.float32),
        ],
        compiler_params=_params("parallel", "parallel"), name="attn_b",
    )(qbt, kb3, vbt)


def _c_window_rows():
    return C_TILE_ROWS + C_ROWS_MAX - 1


def _attn_c_kernel(q_ref, k_ref, v_ref, bias_ref, o_ref, *, rows):
    t = pl.program_id(1)
    win = _c_window_rows() * GRID_W
    ws = jnp.clip(t * C_TILE_ROWS - C_ROWS_MAX // 2, 0, rows - _c_window_rows())
    start = pl.multiple_of(ws * GRID_W, GRID_W)
    kw = k_ref[0, pl.ds(start, win), :]
    vw = v_ref[0, pl.ds(start, win), :]
    q = q_ref[0]
    lane_head = lax.broadcasted_iota(jnp.int32, q.shape, 1) // HEAD_DIM
    out = jnp.zeros(q.shape, jnp.float32)
    for h in range(C_HEADS):
        qh = jnp.where(lane_head == h, q, jnp.zeros_like(q))
        s = lax.dot_general(qh, kw, _NT, preferred_element_type=jnp.float32) + bias_ref[0, h]
        m = jnp.max(s, axis=-1, keepdims=True)
        p = jnp.exp2(s - m)
        l = jnp.sum(p, axis=-1, keepdims=True)
        pv = _dot(_bf16(p), vw)
        out = jnp.where(lane_head == h, pv / l, out)
    o_ref[0] = _bf16(out)


def _c_bias_tables(rpb, rows):
    n_tiles = rows // C_TILE_ROWS
    wr = _c_window_rows()
    kr = min(C_ROWS_MAX, rows)
    c = np.arange(GRID_W)
    c0 = np.clip(c - C_COLS // 2, 0, GRID_W - C_COLS)
    col_ok = (c[None, :] >= c0[:, None]) & (c[None, :] < c0[:, None] + C_COLS)
    dc = np.clip(c[None, :] - c[:, None] + (C_COLS - 1), 0, 2 * C_COLS - 2)
    by_dr = jnp.where(col_ok[None, None], rpb.astype(jnp.float32)[:, :, dc] * LOG2E, NEG_INF)
    masked = jnp.full((rpb.shape[0], GRID_W, GRID_W), NEG_INF, jnp.float32)
    tabs = []
    for t in (0, 1, n_tiles - 1):
        ws = int(np.clip(t * C_TILE_ROWS - C_ROWS_MAX // 2, 0, rows - wr))
        q_blocks = []
        for r in range(t * C_TILE_ROWS, (t + 1) * C_TILE_ROWS):
            r0 = int(np.clip(r - kr // 2, 0, rows - kr))
            k_blocks = [by_dr[:, krow - r + C_ROWS_MAX - 1] if r0 <= krow < r0 + kr else masked
                        for krow in range(ws, ws + wr)]
            q_blocks.append(jnp.concatenate(k_blocks, axis=2))
        tabs.append(jnp.concatenate(q_blocks, axis=1))
    return jnp.stack(tabs)


def _attn_c(qc, kc, vc, bias_tabs, batch, seq):
    rows = seq // GRID_W
    n_tiles = rows // C_TILE_ROWS
    tq = C_TILE_ROWS * GRID_W
    view = lambda t: t.reshape(batch, seq, C_WIDTH)
    cls = lambda b, t: (jnp.where(t == 0, 0, jnp.where(t == n_tiles - 1, 2, 1)), 0, 0, 0)
    out = pl.pallas_call(
        functools.partial(_attn_c_kernel, rows=rows), grid=(batch, n_tiles),
        in_specs=[
            pl.BlockSpec((1, tq, C_WIDTH), lambda b, t: (b, t, 0)),
            pl.BlockSpec((1, seq, C_WIDTH), lambda b, t: (b, 0, 0)),
            pl.BlockSpec((1, seq, C_WIDTH), lambda b, t: (b, 0, 0)),
            pl.BlockSpec((1,) + bias_tabs.shape[1:], cls),
        ],
        out_specs=pl.BlockSpec((1, tq, C_WIDTH), lambda b, t: (b, t, 0)),
        out_shape=jax.ShapeDtypeStruct((batch, seq, C_WIDTH), jnp.bfloat16),
        compiler_params=_params("parallel", "arbitrary"), name="attn_c",
    )(view(qc), view(kc), view(vc), bias_tabs)
    return out.reshape(batch * seq, C_WIDTH)


def _mix_kernel(x_ref, oa_ref, obt_ref, oc_ref, wg_ref, bg_ref, wa_ref, wb_ref, wc_ref, wo_ref,
                g_ref, b_ref, y_ref, *, alpha):
    x = x_ref[...]
    xb = _bf16(x)
    d = x.shape[1]
    projs = (
        _dot(oa_ref[...], wa_ref[...]),
        lax.dot_general(obt_ref[0], wb_ref[...], _TN, preferred_element_type=jnp.float32),
        _dot(oc_ref[...], wc_ref[...]),
    )
    merged = None
    for br, proj in enumerate(projs):
        logits = _dot(xb, wg_ref[:, br * d:(br + 1) * d]) + bg_ref[:, br * d:(br + 1) * d]
        term = jax.nn.sigmoid(logits) * proj
        merged = term if merged is None else merged + term
    mix = _dot(_bf16(merged), wo_ref[...])
    y_ref[...] = _layer_norm(alpha * x + mix, g_ref[...], b_ref[...])


def _mix(x2, oa, obt, oc, wg, bg, wa, wb, wc, wo, g, b, alpha, seq):
    n, d = x2.shape
    tm = ROW_TILE
    nt = seq // tm
    row = lambda i: (i, 0)
    const = lambda i: (0, 0)
    full = lambda a: pl.BlockSpec(a.shape, const)
    return pl.pallas_call(
        functools.partial(_mix_kernel, alpha=alpha), grid=(n // tm,),
        in_specs=[
            pl.BlockSpec((tm, d), row), pl.BlockSpec((tm, A_WIDTH), row),
            pl.BlockSpec((1, B_Q_WIDTH, tm), lambda i: (i // nt, 0, i % nt)),
            pl.BlockSpec((tm, C_WIDTH), row),
            full(wg), full(bg), full(wa), full(wb), full(wc), full(wo), full(g), full(b),
        ],
        out_specs=pl.BlockSpec((tm, d), row),
        out_shape=jax.ShapeDtypeStruct((n, d), jnp.float32),
        compiler_params=_params("parallel"), name="mix",
    )(x2, oa, obt, oc, wg, bg, wa, wb, wc, wo, g, b)


def _mlp_kernel(x_ref, wu_ref, wd_ref, g_ref, b_ref, y_ref, *, alpha, chunk):
    x = x_ref[...]
    xb = _bf16(x)
    ff = None
    for c in range(wu_ref.shape[1] // chunk):
        hid = jnp.square(jnp.maximum(_dot(xb, wu_ref[:, c * chunk:(c + 1) * chunk]), 0.0))
        part = _dot(_bf16(hid), wd_ref[c * chunk:(c + 1) * chunk, :])
        ff = part if ff is None else ff + part
    y_ref[...] = _layer_norm(alpha * x + ff, g_ref[...], b_ref[...])


def _mlp(x2, wu, wd, g, b, alpha):
    n, d = x2.shape
    tm = ROW_TILE
    row = lambda i: (i, 0)
    const = lambda i: (0, 0)
    single = pl.Buffered(1)
    return pl.pallas_call(
        functools.partial(_mlp_kernel, alpha=alpha, chunk=1024), grid=(n // tm,),
        in_specs=[
            pl.BlockSpec((tm, d), row),
            pl.BlockSpec(wu.shape, const, pipeline_mode=single),
            pl.BlockSpec(wd.shape, const, pipeline_mode=single),
            pl.BlockSpec(g.shape, const), pl.BlockSpec(b.shape, const),
        ],
        out_specs=pl.BlockSpec((tm, d), row),
        out_shape=jax.ShapeDtypeStruct((n, d), jnp.float32),
        compiler_params=_params("parallel"), name="mlp",
    )(x2, wu, wd, g, b)


def _rotary_tables(pos_list, theta_list, half, width_per_head, reps):
    s = pos_list[0].shape[0]
    c = jnp.ones((s, width_per_head), jnp.float32)
    sa = jnp.zeros((s, width_per_head), jnp.float32)
    sb = jnp.zeros((s, width_per_head), jnp.float32)
    for i, (pos, theta) in enumerate(zip(pos_list, theta_list)):
        inv = theta ** (-jnp.arange(half, dtype=jnp.float32) / half)
        ang = pos.astype(jnp.float32)[:, None] * inv[None, :]
        cos, sin = jnp.cos(ang), jnp.sin(ang)
        o = 2 * half * i
        c = c.at[:, o:o + half].set(cos).at[:, o + half:o + 2 * half].set(cos)
        sa = sa.at[:, o:o + half].set(-sin)
        sb = sb.at[:, o + half:o + 2 * half].set(sin)
    tile = lambda t: jnp.tile(t, (1, reps))
    return tile(c), tile(sa), tile(sb)


def kernel(x, w_in, b_gate, q_norm_b, k_norm_b, rpb_c, w_branch_a, w_branch_b, w_branch_c, w_out,
           ln1_g, ln1_b, w_up, w_down, ln2_g, ln2_b):
    batch, seq, d = x.shape
    depth = w_in.shape[0]
    alpha = (2 * depth) ** 0.25
    rows = seq // GRID_W
    assert seq % (ROW_TILE) == 0 and rows % C_TILE_ROWS == 0 and rows >= 2 * _c_window_rows()
    assert all((seq // dil) >= Q_BLOCK + window // dil for window, dil in A_CONFIGS)

    pos = jnp.arange(seq)
    tabs_a = _rotary_tables([pos], [A_ROPE_THETA], A_ROPE_DIMS // 2, HEAD_DIM, A_HEADS)
    tabs_b = _rotary_tables([pos // GRID_W, pos % GRID_W], [B_AXIAL_THETA, B_AXIAL_THETA],
                            HEAD_DIM // 4, HEAD_DIM, B_KV_HEADS)
    head_of = np.arange(B_Q_WIDTH) // HEAD_DIM
    ones_blk = jnp.asarray(head_of[:, None] == head_of[None, :], jnp.bfloat16)

    bf = jnp.bfloat16
    x2 = x.reshape(batch * seq, d)
    for layer in range(depth):
        w_qkv = w_in[layer, :, :QKV_COLS].astype(bf)
        w_gate = w_in[layer, :, QKV_COLS:].astype(bf)
        qg = jnp.tile(q_norm_b[layer], B_Q_HEADS)[None, :]
        kg = jnp.tile(k_norm_b[layer], B_KV_HEADS)[None, :]
        qa, ka, va, qbt, kb, vbt, qc, kc, vc = _in_proj(
            x2, w_qkv, tabs_a, tabs_b, qg, kg, ones_blk, batch, seq)
        oa = _attn_a(qa, ka, va, batch, seq)
        obt = _attn_b(qbt, kb, vbt, batch, seq)
        oc = _attn_c(qc, kc, vc, _c_bias_tables(rpb_c[layer], rows), batch, seq)
        x2 = _mix(x2, oa, obt, oc, w_gate, b_gate[layer][None, :],
                  w_branch_a[layer].astype(bf), w_branch_b[layer].astype(bf),
                  w_branch_c[layer].astype(bf), w_out[layer].astype(bf),
                  ln1_g[layer][None, :], ln1_b[layer][None, :], alpha, seq)
        x2 = _mlp(x2, w_up[layer].astype(bf), w_down[layer].astype(bf),
                  ln2_g[layer][None, :], ln2_b[layer][None, :], alpha)
    return x2.reshape(batch, seq, d)
```

```python
import functools
import math

import numpy as np
import jax
import jax.numpy as jnp
from jax import lax
from jax.experimental import pallas as pl
from jax.experimental.pallas import tpu as pltpu

HEAD_DIM = 64
A_HEADS = 4
A_CONFIGS = ((128, 1), (512, 4), (2048, 16))
A_ROPE_DIMS = HEAD_DIM // 4
A_ROPE_THETA = 500000.0
B_Q_HEADS = 8
B_KV_HEADS = 2
B_AXIAL_THETA = 10000.0
C_HEADS = 4
C_ROWS_MAX = 8
C_COLS = 16
GRID_W = 64
Q_BLOCK = 128
LN_EPS = 1e-5
RMS_EPS = 1e-6
NEG_INF = -1e30
LOG2E = math.log2(math.e)

A_WIDTH = A_HEADS * HEAD_DIM
B_Q_WIDTH = B_Q_HEADS * HEAD_DIM
B_KV_WIDTH = B_KV_HEADS * HEAD_DIM
C_WIDTH = C_HEADS * HEAD_DIM
QKV_COLS = 3 * A_WIDTH + B_Q_WIDTH + 2 * B_KV_WIDTH + 3 * C_WIDTH

VMEM_LIMIT_BYTES = 56 * 1024 * 1024
ROW_TILE = 512
B_TILE = 256
B_VT_ROWS = HEAD_DIM + 16
B_UNROLL = 4
C_TILE_ROWS = 4
A_RADIUS = A_CONFIGS[0][0] // (2 * A_CONFIGS[0][1])
A_STEP = Q_BLOCK * max(dil for _, dil in A_CONFIGS)
A_INTERLEAVE = 4

_NT = (((1,), (1,)), ((), ()))
_TN = (((0,), (0,)), ((), ()))


def _params(*sem):
    return pltpu.CompilerParams(dimension_semantics=sem, vmem_limit_bytes=VMEM_LIMIT_BYTES)


def _bf16(x):
    return x.astype(jnp.bfloat16)


def _dot(a, b):
    return jnp.dot(a, b, preferred_element_type=jnp.float32)


def _layer_norm(y, g, b):
    mu = jnp.mean(y, axis=-1, keepdims=True)
    d = y - mu
    var = jnp.mean(d * d, axis=-1, keepdims=True)
    return d * lax.rsqrt(var + LN_EPS) * g + b


def _rotate(x, c, sa, sb, half):
    w = x.shape[-1]
    return x * c + pltpu.roll(x, w - half, 1) * sa + pltpu.roll(x, half, 1) * sb


def _head_mean_sq(h, ones_blk):
    sq = h * h
    hi = _bf16(sq)
    lo = _bf16(sq - hi.astype(jnp.float32))
    return (_dot(hi, ones_blk) + _dot(lo, ones_blk)) * (1.0 / HEAD_DIM)


def _in_proj_kernel(x_ref, w_ref, ca_ref, saa_ref, sba_ref, cb_ref, sab_ref, sbb_ref,
                    qg_ref, kg_ref, ones_ref,
                    a1_ref, a4_ref, a16_ref, qbt_ref, kb_ref, vbt_ref, qc_ref, kc_ref, vc_ref,
                    stage_sc):
    xb = _bf16(x_ref[...])
    scale = HEAD_DIM ** -0.5

    ha = _dot(xb, w_ref[:, 0:3 * A_WIDTH])
    ca, saa, sba = ca_ref[...], saa_ref[...], sba_ref[...]
    half_a = A_ROPE_DIMS // 2
    qa = _rotate(ha[:, 0:A_WIDTH], ca, saa, sba, half_a) * (scale * LOG2E)
    ka = _rotate(ha[:, A_WIDTH:2 * A_WIDTH], ca, saa, sba, half_a)
    qkv_a = (qa, ka, ha[:, 2 * A_WIDTH:3 * A_WIDTH])
    for part, val in enumerate(qkv_a):
        a1_ref[0, :, part * A_WIDTH:(part + 1) * A_WIDTH] = _bf16(val)
        for s in range(A_WIDTH // 128):
            stage_sc[part * (A_WIDTH // 128) + s] = val[:, s * 128:(s + 1) * 128]
    tm = x_ref.shape[0]
    for out_ref in (a4_ref, a16_ref):
        dil = out_ref.shape[1]
        for r in range(dil):
            for s in range(stage_sc.shape[0]):
                out_ref[0, r, :, s * 128:(s + 1) * 128] = _bf16(
                    stage_sc[s, pl.ds(r, tm // dil, stride=dil), :])

    off = 3 * A_WIDTH
    hb = _dot(xb, w_ref[:, off:off + B_Q_WIDTH + 2 * B_KV_WIDTH])
    ones_q = ones_ref[...]
    ones_k = ones_ref[0:B_KV_WIDTH, 0:B_KV_WIDTH]
    cb, sab, sbb = cb_ref[...], sab_ref[...], sbb_ref[...]
    rep = B_Q_WIDTH // B_KV_WIDTH
    cb_q = jnp.concatenate([cb] * rep, axis=1)
    sab_q = jnp.concatenate([sab] * rep, axis=1)
    sbb_q = jnp.concatenate([sbb] * rep, axis=1)
    half_b = HEAD_DIM // 4
    qb = hb[:, 0:B_Q_WIDTH]
    qb = qb * lax.rsqrt(_head_mean_sq(qb, ones_q) + RMS_EPS) * qg_ref[...]
    qb = _rotate(qb, cb_q, sab_q, sbb_q, half_b) * (scale * LOG2E)
    kb = hb[:, B_Q_WIDTH:B_Q_WIDTH + B_KV_WIDTH]
    kb = kb * lax.rsqrt(_head_mean_sq(kb, ones_k) + RMS_EPS) * kg_ref[...]
    kb = _rotate(kb, cb, sab, sbb, half_b)
    vb = hb[:, B_Q_WIDTH + B_KV_WIDTH:]
    qbt_ref[0] = _bf16(qb.T)
    kb_ref[...] = _bf16(kb)
    ones_rows = jnp.ones((B_VT_ROWS - HEAD_DIM, B_TILE), jnp.bfloat16)
    for c in range(vbt_ref.shape[1]):
        vt = _bf16(vb[c * B_TILE:(c + 1) * B_TILE, :].T)
        for g in range(B_KV_HEADS):
            vbt_ref[0, c, g * B_VT_ROWS:g * B_VT_ROWS + HEAD_DIM, :] = vt[g * HEAD_DIM:(g + 1) * HEAD_DIM]
            vbt_ref[0, c, g * B_VT_ROWS + HEAD_DIM:(g + 1) * B_VT_ROWS, :] = ones_rows

    off += B_Q_WIDTH + 2 * B_KV_WIDTH
    hc = _dot(xb, w_ref[:, off:off + 3 * C_WIDTH])
    qc_ref[...] = _bf16(hc[:, 0:C_WIDTH] * (scale * LOG2E))
    kc_ref[...] = _bf16(hc[:, C_WIDTH:2 * C_WIDTH])
    vc_ref[...] = _bf16(hc[:, 2 * C_WIDTH:3 * C_WIDTH])


def _in_proj(x2, w_qkv, tabs_a, tabs_b, qg, kg, ones_blk, batch, seq):
    n, d = x2.shape
    tm = ROW_TILE
    nt = seq // tm
    row = lambda i: (i, 0)
    pos = lambda i: (i % nt, 0)
    const = lambda i: (0, 0)
    bf = jnp.bfloat16
    out_shape = (
        jax.ShapeDtypeStruct((batch, seq, 3 * A_WIDTH), bf),
        *(jax.ShapeDtypeStruct((batch, dil, seq // dil, 3 * A_WIDTH), bf) for _, dil in A_CONFIGS[1:]),
        jax.ShapeDtypeStruct((batch, B_Q_WIDTH, seq), bf),
        jax.ShapeDtypeStruct((n, B_KV_WIDTH), bf),
        jax.ShapeDtypeStruct((batch, seq // B_TILE, B_KV_HEADS * B_VT_ROWS, B_TILE), bf),
        jax.ShapeDtypeStruct((n, C_WIDTH), bf), jax.ShapeDtypeStruct((n, C_WIDTH), bf),
        jax.ShapeDtypeStruct((n, C_WIDTH), bf),
    )
    out_specs = (
        pl.BlockSpec((1, tm, 3 * A_WIDTH), lambda i: (i // nt, i % nt, 0)),
        *(pl.BlockSpec((1, dil, tm // dil, 3 * A_WIDTH), lambda i: (i // nt, 0, i % nt, 0))
          for _, dil in A_CONFIGS[1:]),
        pl.BlockSpec((1, B_Q_WIDTH, tm), lambda i: (i // nt, 0, i % nt)),
        pl.BlockSpec((tm, B_KV_WIDTH), row),
        pl.BlockSpec((1, tm // B_TILE, B_KV_HEADS * B_VT_ROWS, B_TILE),
                     lambda i: (i // nt, i % nt, 0, 0)),
        pl.BlockSpec((tm, C_WIDTH), row), pl.BlockSpec((tm, C_WIDTH), row),
        pl.BlockSpec((tm, C_WIDTH), row),
    )
    in_specs = [
        pl.BlockSpec((tm, d), row),
        pl.BlockSpec(w_qkv.shape, const),
        pl.BlockSpec((tm, A_WIDTH), pos), pl.BlockSpec((tm, A_WIDTH), pos),
        pl.BlockSpec((tm, A_WIDTH), pos),
        pl.BlockSpec((tm, B_KV_WIDTH), pos), pl.BlockSpec((tm, B_KV_WIDTH), pos),
        pl.BlockSpec((tm, B_KV_WIDTH), pos),
        pl.BlockSpec(qg.shape, const), pl.BlockSpec(kg.shape, const),
        pl.BlockSpec(ones_blk.shape, const),
    ]
    return pl.pallas_call(
        _in_proj_kernel, grid=(n // tm,), in_specs=in_specs, out_specs=out_specs,
        out_shape=out_shape,
        scratch_shapes=[pltpu.VMEM((3 * A_WIDTH // 128, tm, 128), jnp.float32)],
        compiler_params=_params("parallel"), name="in_proj",
    )(x2, w_qkv, *tabs_a, *tabs_b, qg, kg, ones_blk)


def _attn_a_kernel(*refs, seq):
    n_cfg = len(A_CONFIGS)
    qkv_refs = [refs[3 * c:3 * c + 3] for c in range(n_cfg)]
    bias_ref, hmask_ref, o_ref, o_sc, m_sc, l_sc = refs[3 * n_cfg:]
    i = pl.program_id(1)
    span = Q_BLOCK + 2 * A_RADIUS
    lane_head = lax.broadcasted_iota(jnp.int32, (Q_BLOCK, A_WIDTH), 1) // HEAD_DIM

    def widen(t, width):
        return t if t.shape[1] in (1, width) else jnp.concatenate([t] * (width // t.shape[1]), axis=1)

    def per_head_lanes(stacked):
        tiles = [widen(stacked[h * Q_BLOCK:(h + 1) * Q_BLOCK], A_WIDTH) for h in range(A_HEADS)]
        out = tiles[-1]
        for h in range(A_HEADS - 2, -1, -1):
            out = jnp.where(lane_head == h, tiles[h], out)
        return out

    def scores(blk):
        qm = jnp.concatenate([blk["q"] * hmask_ref[h] for h in range(A_HEADS)], axis=0)
        return lax.dot_general(qm, blk["kw"], _NT, preferred_element_type=jnp.float32)

    def softmax(blk, s, first):
        rows = blk["rows"]
        s = s + jnp.concatenate([bias_ref[blk["edge"]]] * A_HEADS, axis=0)
        m_new = jnp.max(s, axis=-1, keepdims=True)
        alpha = None
        if not first:
            m_prev = jnp.concatenate([m_sc[h, rows, :] for h in range(A_HEADS)], axis=0)
            l_prev = jnp.concatenate([l_sc[h, rows, :] for h in range(A_HEADS)], axis=0)
            m_new = jnp.maximum(m_prev, m_new)
            alpha = jnp.exp2(m_prev - m_new)
        p = jnp.exp2(s - widen(m_new, span))
        l_new = jnp.sum(p, axis=-1, keepdims=True)
        if not first:
            l_new = l_new + alpha * l_prev
        return _bf16(p), m_new, l_new, alpha

    def update(blk, pv, m_new, l_new, alpha, first, last):
        rows = blk["rows"]
        o_new = per_head_lanes(pv)
        if not first:
            o_prev = jnp.concatenate([o_sc[s, rows, :] for s in range(o_sc.shape[0])], axis=1)
            o_new = o_new + per_head_lanes(alpha) * o_prev
        if last:
            o_new = o_new / per_head_lanes(l_new)
        else:
            for h in range(A_HEADS):
                hs = slice(h * Q_BLOCK, (h + 1) * Q_BLOCK)
                m_sc[h, rows, :] = jnp.broadcast_to(m_new[hs], (Q_BLOCK, 128))
                l_sc[h, rows, :] = jnp.broadcast_to(l_new[hs], (Q_BLOCK, 128))
        for s in range(o_sc.shape[0]):
            o_sc[s, rows, :] = o_new[:, s * 128:(s + 1) * 128]

    def process(blocks, first, last):
        ss = [scores(blk) for blk in blocks]
        res = []
        for blk, s in zip(blocks, ss):
            p, m_new, l_new, alpha = softmax(blk, s, first)
            res.append((_dot(p, blk["vw"]), m_new, l_new, alpha))
        for blk, (pv, m_new, l_new, alpha) in zip(blocks, res):
            update(blk, pv, m_new, l_new, alpha, first, last)

    for ci, (_, dil) in enumerate(A_CONFIGS):
        q_ref, k_ref, v_ref = qkv_refs[ci]
        sub_len = seq // dil
        blocks_per_res = A_STEP // (dil * Q_BLOCK)
        first, last = ci == 0, ci == n_cfg - 1

        def block(idx, dil=dil, q_ref=q_ref, k_ref=k_ref, v_ref=v_ref, sub_len=sub_len,
                  blocks_per_res=blocks_per_res):
            r, qb = idx // blocks_per_res, idx % blocks_per_res
            base = i * (A_STEP // dil) + qb * Q_BLOCK
            start = pl.multiple_of(jnp.clip(base - A_RADIUS, 0, sub_len - span), A_RADIUS)
            q_rows = pl.ds(pl.multiple_of(qb * Q_BLOCK, Q_BLOCK), Q_BLOCK)
            if dil == 1:
                q, kw, vw = q_ref[0, q_rows, :], k_ref[0, pl.ds(start, span), :], v_ref[0, pl.ds(start, span), :]
                rows = q_rows
            else:
                q, kw, vw = (q_ref[0, r, q_rows, :], k_ref[0, r, pl.ds(start, span), :],
                             v_ref[0, r, pl.ds(start, span), :])
                rows = pl.ds(qb * Q_BLOCK * dil + r, Q_BLOCK, stride=dil)
            return dict(q=q, kw=kw, vw=vw, rows=rows, edge=(base - start) // A_RADIUS)

        def pair(t, carry, block=block, first=first, last=last):
            process([block(A_INTERLEAVE * t + u) for u in range(A_INTERLEAVE)], first, last)
            return carry

        lax.fori_loop(0, A_STEP // Q_BLOCK // A_INTERLEAVE, pair, 0)

    o_ref[0] = _bf16(jnp.concatenate([o_sc[s] for s in range(o_sc.shape[0])], axis=1))


def _attn_a(a_qkv, batch, seq):
    single = pl.Buffered(1)
    args, in_specs = [], []
    for (_, dil), arr in zip(A_CONFIGS, a_qkv):
        sub_len = seq // dil
        if dil == 1:
            q_spec = pl.BlockSpec((1, A_STEP, A_WIDTH), lambda b, i: (b, i, 0))
            kv_spec = lambda part: pl.BlockSpec((1, seq, A_WIDTH), lambda b, i: (b, 0, part),
                                                pipeline_mode=single)
        else:
            q_spec = pl.BlockSpec((1, dil, A_STEP // dil, A_WIDTH), lambda b, i: (b, 0, i, 0))
            kv_spec = lambda part, dil=dil, sub_len=sub_len: pl.BlockSpec(
                (1, dil, sub_len, A_WIDTH), lambda b, i: (b, 0, 0, part), pipeline_mode=single)
        args += [arr, arr, arr]
        in_specs += [q_spec, kv_spec(1), kv_spec(2)]
    span = Q_BLOCK + 2 * A_RADIUS
    rel = np.arange(Q_BLOCK)[:, None] - np.arange(span)[None, :]
    bias = jnp.asarray(np.stack([np.where(np.abs(rel + e * A_RADIUS) <= A_RADIUS, 0.0, NEG_INF)
                                 for e in range(3)]), jnp.float32)
    hmask = jnp.asarray((np.arange(A_WIDTH)[None, :] // HEAD_DIM == np.arange(A_HEADS)[:, None])[:, None, :],
                        jnp.bfloat16)
    const3 = lambda b, i: (0, 0, 0)
    out = pl.pallas_call(
        functools.partial(_attn_a_kernel, seq=seq), grid=(batch, seq // A_STEP),
        in_specs=in_specs + [pl.BlockSpec(bias.shape, const3), pl.BlockSpec(hmask.shape, const3)],
        out_specs=pl.BlockSpec((1, A_STEP, A_WIDTH), lambda b, i: (b, i, 0)),
        out_shape=jax.ShapeDtypeStruct((batch, seq, A_WIDTH), jnp.bfloat16),
        scratch_shapes=[pltpu.VMEM((A_WIDTH // 128, A_STEP, 128), jnp.float32),
                        pltpu.VMEM((A_HEADS, A_STEP, 128), jnp.float32),
                        pltpu.VMEM((A_HEADS, A_STEP, 128), jnp.float32)],
        compiler_params=_params("parallel", "arbitrary"), name="attn_a",
    )(*args, bias, hmask)
    return out.reshape(batch * seq, A_WIDTH)


def _attn_b_kernel(qt_ref, k_ref, vt_ref, ot_ref, wq_sc, m_sc, acc_sc, st_sc):
    assert B_UNROLL % 2 == 0 and vt_ref.shape[1] % B_UNROLL == 0
    tq = qt_ref.shape[2]
    n_kt = vt_ref.shape[1]
    group = B_Q_HEADS // B_KV_HEADS
    wq_sc[...] = jnp.zeros(wq_sc.shape, wq_sc.dtype)
    for h in range(B_Q_HEADS):
        g, hh = divmod(h, group)
        wq_sc[g, g * HEAD_DIM:(g + 1) * HEAD_DIM, hh * tq:(hh + 1) * tq] = (
            qt_ref[0, h * HEAD_DIM:(h + 1) * HEAD_DIM, :])
    m_sc[...] = jnp.full(m_sc.shape, NEG_INF, jnp.float32)
    acc_sc[...] = jnp.zeros(acc_sc.shape, jnp.float32)

    def k_tile(j):
        return k_ref[0, pl.ds(pl.multiple_of(j * B_TILE, B_TILE), B_TILE), :]

    def softmax_pv(j, g, st):
        m_prev = m_sc[g]
        m_new = jnp.maximum(m_prev, jnp.max(st, axis=0, keepdims=True))
        alpha = jnp.exp2(m_prev - m_new)
        p = jnp.exp2(st - m_new)
        vt = vt_ref[0, j, g * B_VT_ROWS:(g + 1) * B_VT_ROWS, :]
        acc_sc[g] = alpha * acc_sc[g] + _dot(vt, _bf16(p))
        m_sc[g] = m_new

    for g in range(B_KV_HEADS):
        st_sc[0, g] = _dot(k_tile(0), wq_sc[g])

    def step(j, slot):
        j_next = jnp.minimum(j + 1, n_kt - 1)
        for g in range(B_KV_HEADS):
            st_sc[1 - slot, g] = _dot(k_tile(j_next), wq_sc[g])
            softmax_pv(j, g, st_sc[slot, g])

    def body(jj, carry):
        for u in range(B_UNROLL):
            step(B_UNROLL * jj + u, u % 2)
        return carry

    lax.fori_loop(0, n_kt // B_UNROLL, body, 0)
    for h in range(B_Q_HEADS):
        g, hh = divmod(h, group)
        sl = slice(hh * tq, (hh + 1) * tq)
        ot_ref[0, h * HEAD_DIM:(h + 1) * HEAD_DIM, :] = _bf16(
            acc_sc[g, 0:HEAD_DIM, sl] / acc_sc[g, HEAD_DIM:HEAD_DIM + 1, sl])


def _attn_b(qbt, kb, vbt, batch, seq):
    tq = B_TILE
    group = B_Q_HEADS // B_KV_HEADS
    kb3 = kb.reshape(batch, seq, B_KV_WIDTH)
    return pl.pallas_call(
        _attn_b_kernel, grid=(batch, seq // tq),
        in_specs=[
            pl.BlockSpec((1, B_Q_WIDTH, tq), lambda b, i: (b, 0, i)),
            pl.BlockSpec((1, seq, B_KV_WIDTH), lambda b, i: (b, 0, 0)),
            pl.BlockSpec((1, seq // B_TILE, B_KV_HEADS * B_VT_ROWS, B_TILE), lambda b, i: (b, 0, 0, 0)),
        ],
        out_specs=pl.BlockSpec((1, B_Q_WIDTH, tq), lambda b, i: (b, 0, i)),
        out_shape=jax.ShapeDtypeStruct((batch, B_Q_WIDTH, seq), jnp.bfloat16),
        scratch_shapes=[
            pltpu.VMEM((B_KV_HEADS, B_KV_WIDTH, group * tq), jnp.bfloat16),
            pltpu.VMEM((B_KV_HEADS, 1, group * tq), jnp.float32),
            pltpu.VMEM((B_KV_HEADS, B_VT_ROWS, group * tq), jnp.float32),
            pltpu.VMEM((2, B_KV_HEADS, B_TILE, group * tq), jnp.float32),
        ],
        compiler_params=_params("parallel", "parallel"), name="attn_b",
    )(qbt, kb3, vbt)


def _c_window_rows():
    return C_TILE_ROWS + C_ROWS_MAX - 1


def _attn_c_kernel(q_ref, k_ref, v_ref, bias_ref, o_ref, *, rows):
    t = pl.program_id(1)
    win = _c_window_rows() * GRID_W
    ws = jnp.clip(t * C_TILE_ROWS - C_ROWS_MAX // 2, 0, rows - _c_window_rows())
    start = pl.multiple_of(ws * GRID_W, GRID_W)
    kw = k_ref[0, pl.ds(start, win), :]
    vw = v_ref[0, pl.ds(start, win), :]
    q = q_ref[0]
    lane_head = lax.broadcasted_iota(jnp.int32, q.shape, 1) // HEAD_DIM
    out = jnp.zeros(q.shape, jnp.float32)
    for h in range(C_HEADS):
        qh = jnp.where(lane_head == h, q, jnp.zeros_like(q))
        s = lax.dot_general(qh, kw, _NT, preferred_element_type=jnp.float32) + bias_ref[0, h]
        m = jnp.max(s, axis=-1, keepdims=True)
        p = jnp.exp2(s - m)
        l = jnp.sum(p, axis=-1, keepdims=True)
        pv = _dot(_bf16(p), vw)
        out = jnp.where(lane_head == h, pv / l, out)
    o_ref[0] = _bf16(out)


def _c_bias_tables(rpb, rows):
    n_tiles = rows // C_TILE_ROWS
    wr = _c_window_rows()
    kr = min(C_ROWS_MAX, rows)
    c = np.arange(GRID_W)
    c0 = np.clip(c - C_COLS // 2, 0, GRID_W - C_COLS)
    col_ok = (c[None, :] >= c0[:, None]) & (c[None, :] < c0[:, None] + C_COLS)
    dc = np.clip(c[None, :] - c[:, None] + (C_COLS - 1), 0, 2 * C_COLS - 2)
    by_dr = jnp.where(col_ok[None, None], rpb.astype(jnp.float32)[:, :, dc] * LOG2E, NEG_INF)
    masked = jnp.full((rpb.shape[0], GRID_W, GRID_W), NEG_INF, jnp.float32)
    tabs = []
    for t in (0, 1, n_tiles - 1):
        ws = int(np.clip(t * C_TILE_ROWS - C_ROWS_MAX // 2, 0, rows - wr))
        q_blocks = []
        for r in range(t * C_TILE_ROWS, (t + 1) * C_TILE_ROWS):
            r0 = int(np.clip(r - kr // 2, 0, rows - kr))
            k_blocks = [by_dr[:, krow - r + C_ROWS_MAX - 1] if r0 <= krow < r0 + kr else masked
                        for krow in range(ws, ws + wr)]
            q_blocks.append(jnp.concatenate(k_blocks, axis=2))
        tabs.append(jnp.concatenate(q_blocks, axis=1))
    return jnp.stack(tabs)


def _attn_c(qc, kc, vc, bias_tabs, batch, seq):
    rows = seq // GRID_W
    n_tiles = rows // C_TILE_ROWS
    tq = C_TILE_ROWS * GRID_W
    view = lambda t: t.reshape(batch, seq, C_WIDTH)
    cls = lambda b, t: (jnp.where(t == 0, 0, jnp.where(t == n_tiles - 1, 2, 1)), 0, 0, 0)
    out = pl.pallas_call(
        functools.partial(_attn_c_kernel, rows=rows), grid=(batch, n_tiles),
        in_specs=[
            pl.BlockSpec((1, tq, C_WIDTH), lambda b, t: (b, t, 0)),
            pl.BlockSpec((1, seq, C_WIDTH), lambda b, t: (b, 0, 0)),
            pl.BlockSpec((1, seq, C_WIDTH), lambda b, t: (b, 0, 0)),
            pl.BlockSpec((1,) + bias_tabs.shape[1:], cls),
        ],
        out_specs=pl.BlockSpec((1, tq, C_WIDTH), lambda b, t: (b, t, 0)),
        out_shape=jax.ShapeDtypeStruct((batch, seq, C_WIDTH), jnp.bfloat16),
        compiler_params=_params("parallel", "arbitrary"), name="attn_c",
    )(view(qc), view(kc), view(vc), bias_tabs)
    return out.reshape(batch * seq, C_WIDTH)


def _mix_kernel(x_ref, oa_ref, obt_ref, oc_ref, wg_ref, bg_ref, wa_ref, wb_ref, wc_ref, wo_ref,
                g_ref, b_ref, y_ref, *, alpha):
    x = x_ref[...]
    xb = _bf16(x)
    d = x.shape[1]
    projs = (
        _dot(oa_ref[...], wa_ref[...]),
        lax.dot_general(obt_ref[0], wb_ref[...], _TN, preferred_element_type=jnp.float32),
        _dot(oc_ref[...], wc_ref[...]),
    )
    merged = None
    for br, proj in enumerate(projs):
        logits = _dot(xb, wg_ref[:, br * d:(br + 1) * d]) + bg_ref[:, br * d:(br + 1) * d]
        term = jax.nn.sigmoid(logits) * proj
        merged = term if merged is None else merged + term
    mix = _dot(_bf16(merged), wo_ref[...])
    y_ref[...] = _layer_norm(alpha * x + mix, g_ref[...], b_ref[...])


def _mix(x2, oa, obt, oc, wg, bg, wa, wb, wc, wo, g, b, alpha, seq):
    n, d = x2.shape
    tm = ROW_TILE
    nt = seq // tm
    row = lambda i: (i, 0)
    const = lambda i: (0, 0)
    full = lambda a: pl.BlockSpec(a.shape, const)
    return pl.pallas_call(
        functools.partial(_mix_kernel, alpha=alpha), grid=(n // tm,),
        in_specs=[
            pl.BlockSpec((tm, d), row), pl.BlockSpec((tm, A_WIDTH), row),
            pl.BlockSpec((1, B_Q_WIDTH, tm), lambda i: (i // nt, 0, i % nt)),
            pl.BlockSpec((tm, C_WIDTH), row),
            full(wg), full(bg), full(wa), full(wb), full(wc), full(wo), full(g), full(b),
        ],
        out_specs=pl.BlockSpec((tm, d), row),
        out_shape=jax.ShapeDtypeStruct((n, d), jnp.float32),
        compiler_params=_params("parallel"), name="mix",
    )(x2, oa, obt, oc, wg, bg, wa, wb, wc, wo, g, b)


def _mlp_kernel(x_ref, wu_ref, wd_ref, g_ref, b_ref, y_ref, *, alpha, chunk):
    x = x_ref[...]
    xb = _bf16(x)
    ff = None
    for c in range(wu_ref.shape[1] // chunk):
        hid = jnp.square(jnp.maximum(_dot(xb, wu_ref[:, c * chunk:(c + 1) * chunk]), 0.0))
        part = _dot(_bf16(hid), wd_ref[c * chunk:(c + 1) * chunk, :])
        ff = part if ff is None else ff + part
    y_ref[...] = _layer_norm(alpha * x + ff, g_ref[...], b_ref[...])


def _mlp(x2, wu, wd, g, b, alpha):
    n, d = x2.shape
    tm = ROW_TILE
    row = lambda i: (i, 0)
    const = lambda i: (0, 0)
    single = pl.Buffered(1)
    return pl.pallas_call(
        functools.partial(_mlp_kernel, alpha=alpha, chunk=1024), grid=(n // tm,),
        in_specs=[
            pl.BlockSpec((tm, d), row),
            pl.BlockSpec(wu.shape, const, pipeline_mode=single),
            pl.BlockSpec(wd.shape, const, pipeline_mode=single),
            pl.BlockSpec(g.shape, const), pl.BlockSpec(b.shape, const),
        ],
        out_specs=pl.BlockSpec((tm, d), row),
        out_shape=jax.ShapeDtypeStruct((n, d), jnp.float32),
        compiler_params=_params("parallel"), name="mlp",
    )(x2, wu, wd, g, b)


def _rotary_tables(pos_list, theta_list, half, width_per_head, reps):
    s = pos_list[0].shape[0]
    c = jnp.ones((s, width_per_head), jnp.float32)
    sa = jnp.zeros((s, width_per_head), jnp.float32)
    sb = jnp.zeros((s, width_per_head), jnp.float32)
    for i, (pos, theta) in enumerate(zip(pos_list, theta_list)):
        inv = theta ** (-jnp.arange(half, dtype=jnp.float32) / half)
        ang = pos.astype(jnp.float32)[:, None] * inv[None, :]
        cos, sin = jnp.cos(ang), jnp.sin(ang)
        o = 2 * half * i
        c = c.at[:, o:o + half].set(cos).at[:, o + half:o + 2 * half].set(cos)
        sa = sa.at[:, o:o + half].set(-sin)
        sb = sb.at[:, o + half:o + 2 * half].set(sin)
    tile = lambda t: jnp.tile(t, (1, reps))
    return tile(c), tile(sa), tile(sb)


def kernel(x, w_in, b_gate, q_norm_b, k_norm_b, rpb_c, w_branch_a, w_branch_b, w_branch_c, w_out,
           ln1_g, ln1_b, w_up, w_down, ln2_g, ln2_b):
    batch, seq, d = x.shape
    depth = w_in.shape[0]
    alpha = (2 * depth) ** 0.25
    rows = seq // GRID_W
    assert seq % (ROW_TILE) == 0 and rows % C_TILE_ROWS == 0 and rows >= 2 * _c_window_rows()
    assert all(window // (2 * dil) == A_RADIUS and (seq // dil) >= Q_BLOCK + 2 * A_RADIUS
               for window, dil in A_CONFIGS)
    assert A_CONFIGS[0][1] == 1 and seq % A_STEP == 0 and len(A_CONFIGS) == 3

    pos = jnp.arange(seq)
    tabs_a = _rotary_tables([pos], [A_ROPE_THETA], A_ROPE_DIMS // 2, HEAD_DIM, A_HEADS)
    tabs_b = _rotary_tables([pos // GRID_W, pos % GRID_W], [B_AXIAL_THETA, B_AXIAL_THETA],
                            HEAD_DIM // 4, HEAD_DIM, B_KV_HEADS)
    head_of = np.arange(B_Q_WIDTH) // HEAD_DIM
    ones_blk = jnp.asarray(head_of[:, None] == head_of[None, :], jnp.bfloat16)

    bf = jnp.bfloat16
    x2 = x.reshape(batch * seq, d)
    for layer in range(depth):
        w_qkv = w_in[layer, :, :QKV_COLS].astype(bf)
        w_gate = w_in[layer, :, QKV_COLS:].astype(bf)
        qg = jnp.tile(q_norm_b[layer], B_Q_HEADS)[None, :]
        kg = jnp.tile(k_norm_b[layer], B_KV_HEADS)[None, :]
        a1, a4, a16, qbt, kb, vbt, qc, kc, vc = _in_proj(
            x2, w_qkv, tabs_a, tabs_b, qg, kg, ones_blk, batch, seq)
        oa = _attn_a((a1, a4, a16), batch, seq)
        obt = _attn_b(qbt, kb, vbt, batch, seq)
        oc = _attn_c(qc, kc, vc, _c_bias_tables(rpb_c[layer], rows), batch, seq)
        x2 = _mix(x2, oa, obt, oc, w_gate, b_gate[layer][None, :],
                  w_branch_a[layer].astype(bf), w_branch_b[layer].astype(bf),
                  w_branch_c[layer].astype(bf), w_out[layer].astype(bf),
                  ln1_g[layer][None, :], ln1_b[layer][None, :], alpha, seq)
        x2 = _mlp(x2, w_up[layer].astype(bf), w_down[layer].astype(bf),
                  ln2_g[layer][None, :], ln2_b[layer][None, :], alpha)
    return x2.reshape(batch, seq, d)
```

```python
import functools
import math

import numpy as np
import jax
import jax.numpy as jnp
from jax import lax
from jax.experimental import pallas as pl
from jax.experimental.pallas import tpu as pltpu

HEAD_DIM = 64
A_HEADS = 4
A_CONFIGS = ((128, 1), (512, 4), (2048, 16))
A_ROPE_DIMS = HEAD_DIM // 4
A_ROPE_THETA = 500000.0
B_Q_HEADS = 8
B_KV_HEADS = 2
B_AXIAL_THETA = 10000.0
C_HEADS = 4
C_ROWS_MAX = 8
C_COLS = 16
GRID_W = 64
Q_BLOCK = 128
LN_EPS = 1e-5
RMS_EPS = 1e-6
NEG_INF = -1e30
LOG2E = math.log2(math.e)

A_WIDTH = A_HEADS * HEAD_DIM
B_Q_WIDTH = B_Q_HEADS * HEAD_DIM
B_KV_WIDTH = B_KV_HEADS * HEAD_DIM
C_WIDTH = C_HEADS * HEAD_DIM
QKV_COLS = 3 * A_WIDTH + B_Q_WIDTH + 2 * B_KV_WIDTH + 3 * C_WIDTH

VMEM_LIMIT_BYTES = 56 * 1024 * 1024
ROW_TILE = 512
B_TILE = 256
B_VT_ROWS = HEAD_DIM + 16
B_UNROLL = 4
C_TILE_ROWS = 4
A_RADIUS = A_CONFIGS[0][0] // (2 * A_CONFIGS[0][1])
A_STEP = Q_BLOCK * max(dil for _, dil in A_CONFIGS)
A_INTERLEAVE = 4

_NT = (((1,), (1,)), ((), ()))
_TN = (((0,), (0,)), ((), ()))


def _params(*sem):
    return pltpu.CompilerParams(dimension_semantics=sem, vmem_limit_bytes=VMEM_LIMIT_BYTES)


def _bf16(x):
    return x.astype(jnp.bfloat16)


def _dot(a, b):
    return jnp.dot(a, b, preferred_element_type=jnp.float32)


def _layer_norm(y, g, b):
    mu = jnp.mean(y, axis=-1, keepdims=True)
    d = y - mu
    var = jnp.mean(d * d, axis=-1, keepdims=True)
    return d * lax.rsqrt(var + LN_EPS) * g + b


def _rotate(x, c, sa, sb, half):
    w = x.shape[-1]
    return x * c + pltpu.roll(x, w - half, 1) * sa + pltpu.roll(x, half, 1) * sb


def _head_mean_sq(h, ones_blk):
    sq = h * h
    hi = _bf16(sq)
    lo = _bf16(sq - hi.astype(jnp.float32))
    return (_dot(hi, ones_blk) + _dot(lo, ones_blk)) * (1.0 / HEAD_DIM)


def _in_proj_kernel(x_ref, w_ref, ca_ref, saa_ref, sba_ref, cb_ref, sab_ref, sbb_ref,
                    qg_ref, kg_ref, ones_ref,
                    a1_ref, a4_ref, a16_ref, qbt_ref, kb_ref, vbt_ref, qc_ref, kc_ref, vc_ref,
                    stage_sc):
    xb = _bf16(x_ref[...])
    scale = HEAD_DIM ** -0.5

    off_b = 3 * A_WIDTH
    off_c = off_b + B_Q_WIDTH + 2 * B_KV_WIDTH

    ha = _dot(xb, w_ref[:, 0:off_b])
    ca, saa, sba = ca_ref[...], saa_ref[...], sba_ref[...]
    half_a = A_ROPE_DIMS // 2
    qa = _rotate(ha[:, 0:A_WIDTH], ca, saa, sba, half_a) * (scale * LOG2E)
    ka = _rotate(ha[:, A_WIDTH:2 * A_WIDTH], ca, saa, sba, half_a)
    qkv_a = (qa, ka, ha[:, 2 * A_WIDTH:3 * A_WIDTH])
    for part, val in enumerate(qkv_a):
        a1_ref[0, :, part * A_WIDTH:(part + 1) * A_WIDTH] = _bf16(val)
        for s in range(A_WIDTH // 128):
            stage_sc[part * (A_WIDTH // 128) + s] = val[:, s * 128:(s + 1) * 128]
    tm = x_ref.shape[0]
    for out_ref in (a4_ref, a16_ref):
        dil = out_ref.shape[1]
        for r in range(dil):
            for s in range(stage_sc.shape[0]):
                out_ref[0, r, :, s * 128:(s + 1) * 128] = _bf16(
                    stage_sc[s, pl.ds(r, tm // dil, stride=dil), :])

    hb = _dot(xb, w_ref[:, off_b:off_c])
    ones_q = ones_ref[...]
    ones_k = ones_ref[0:B_KV_WIDTH, 0:B_KV_WIDTH]
    cb, sab, sbb = cb_ref[...], sab_ref[...], sbb_ref[...]
    rep = B_Q_WIDTH // B_KV_WIDTH
    cb_q = jnp.concatenate([cb] * rep, axis=1)
    sab_q = jnp.concatenate([sab] * rep, axis=1)
    sbb_q = jnp.concatenate([sbb] * rep, axis=1)
    half_b = HEAD_DIM // 4
    qb = hb[:, 0:B_Q_WIDTH]
    qb = qb * lax.rsqrt(_head_mean_sq(qb, ones_q) + RMS_EPS) * qg_ref[...]
    qb = _rotate(qb, cb_q, sab_q, sbb_q, half_b) * (scale * LOG2E)
    kb = hb[:, B_Q_WIDTH:B_Q_WIDTH + B_KV_WIDTH]
    kb = kb * lax.rsqrt(_head_mean_sq(kb, ones_k) + RMS_EPS) * kg_ref[...]
    kb = _rotate(kb, cb, sab, sbb, half_b)
    vb = hb[:, B_Q_WIDTH + B_KV_WIDTH:]
    qbt_ref[0] = _bf16(qb.T)
    kb_ref[...] = _bf16(kb)
    ones_rows = jnp.ones((B_VT_ROWS - HEAD_DIM, B_TILE), jnp.bfloat16)
    for c in range(vbt_ref.shape[1]):
        vt = _bf16(vb[c * B_TILE:(c + 1) * B_TILE, :].T)
        for g in range(B_KV_HEADS):
            vbt_ref[0, c, g * B_VT_ROWS:g * B_VT_ROWS + HEAD_DIM, :] = vt[g * HEAD_DIM:(g + 1) * HEAD_DIM]
            vbt_ref[0, c, g * B_VT_ROWS + HEAD_DIM:(g + 1) * B_VT_ROWS, :] = ones_rows

    hc = _dot(xb, w_ref[:, off_c:off_c + 3 * C_WIDTH])
    qc_ref[...] = _bf16(hc[:, 0:C_WIDTH] * (scale * LOG2E))
    kc_ref[...] = _bf16(hc[:, C_WIDTH:2 * C_WIDTH])
    vc_ref[...] = _bf16(hc[:, 2 * C_WIDTH:3 * C_WIDTH])


def _layer_spec(stacked, layer):
    zeros = (0,) * (stacked.ndim - 1)
    return pl.BlockSpec((None,) + stacked.shape[1:], lambda *_: (layer,) + zeros,
                        pipeline_mode=pl.Buffered(1))


def _in_proj(x2, w_in, layer, tabs_a, tabs_b, qg, kg, ones_blk, batch, seq):
    n, d = x2.shape
    tm = ROW_TILE
    nt = seq // tm
    row = lambda i: (i, 0)
    pos = lambda i: (i % nt, 0)
    const = lambda i: (0, 0)
    bf = jnp.bfloat16
    out_shape = (
        jax.ShapeDtypeStruct((batch, seq, 3 * A_WIDTH), bf),
        *(jax.ShapeDtypeStruct((batch, dil, seq // dil, 3 * A_WIDTH), bf) for _, dil in A_CONFIGS[1:]),
        jax.ShapeDtypeStruct((batch, B_Q_WIDTH, seq), bf),
        jax.ShapeDtypeStruct((n, B_KV_WIDTH), bf),
        jax.ShapeDtypeStruct((batch, seq // B_TILE, B_KV_HEADS * B_VT_ROWS, B_TILE), bf),
        jax.ShapeDtypeStruct((n, C_WIDTH), bf), jax.ShapeDtypeStruct((n, C_WIDTH), bf),
        jax.ShapeDtypeStruct((n, C_WIDTH), bf),
    )
    out_specs = (
        pl.BlockSpec((1, tm, 3 * A_WIDTH), lambda i: (i // nt, i % nt, 0)),
        *(pl.BlockSpec((1, dil, tm // dil, 3 * A_WIDTH), lambda i: (i // nt, 0, i % nt, 0))
          for _, dil in A_CONFIGS[1:]),
        pl.BlockSpec((1, B_Q_WIDTH, tm), lambda i: (i // nt, 0, i % nt)),
        pl.BlockSpec((tm, B_KV_WIDTH), row),
        pl.BlockSpec((1, tm // B_TILE, B_KV_HEADS * B_VT_ROWS, B_TILE),
                     lambda i: (i // nt, i % nt, 0, 0)),
        pl.BlockSpec((tm, C_WIDTH), row), pl.BlockSpec((tm, C_WIDTH), row),
        pl.BlockSpec((tm, C_WIDTH), row),
    )
    in_specs = [
        pl.BlockSpec((tm, d), row),
        _layer_spec(w_in, layer),
        pl.BlockSpec((tm, A_WIDTH), pos), pl.BlockSpec((tm, A_WIDTH), pos),
        pl.BlockSpec((tm, A_WIDTH), pos),
        pl.BlockSpec((tm, B_KV_WIDTH), pos), pl.BlockSpec((tm, B_KV_WIDTH), pos),
        pl.BlockSpec((tm, B_KV_WIDTH), pos),
        pl.BlockSpec(qg.shape, const), pl.BlockSpec(kg.shape, const),
        pl.BlockSpec(ones_blk.shape, const),
    ]
    return pl.pallas_call(
        _in_proj_kernel, grid=(n // tm,), in_specs=in_specs, out_specs=out_specs,
        out_shape=out_shape,
        scratch_shapes=[pltpu.VMEM((3 * A_WIDTH // 128, tm, 128), jnp.float32)],
        compiler_params=_params("parallel"), name="in_proj",
    )(x2, w_in, *tabs_a, *tabs_b, qg, kg, ones_blk)


def _attn_a_kernel(*refs, seq):
    n_cfg = len(A_CONFIGS)
    qkv_refs = [refs[3 * c:3 * c + 3] for c in range(n_cfg)]
    bias_ref, hmask_ref, o_ref, o_sc, m_sc, l_sc = refs[3 * n_cfg:]
    i = pl.program_id(1)
    span = Q_BLOCK + 2 * A_RADIUS
    lane_head = lax.broadcasted_iota(jnp.int32, (Q_BLOCK, A_WIDTH), 1) // HEAD_DIM

    def widen(t, width):
        return t if t.shape[1] in (1, width) else jnp.concatenate([t] * (width // t.shape[1]), axis=1)

    def per_head_lanes(stacked):
        tiles = [widen(stacked[h * Q_BLOCK:(h + 1) * Q_BLOCK], A_WIDTH) for h in range(A_HEADS)]
        out = tiles[-1]
        for h in range(A_HEADS - 2, -1, -1):
            out = jnp.where(lane_head == h, tiles[h], out)
        return out

    def scores(blk):
        halves = []
        for h0 in range(0, A_HEADS, A_HEADS // 2):
            qm = jnp.concatenate([blk["q"] * hmask_ref[h] for h in range(h0, h0 + A_HEADS // 2)], axis=0)
            halves.append(lax.dot_general(qm, blk["kw"], _NT, preferred_element_type=jnp.float32))
        return jnp.concatenate(halves, axis=0)

    def softmax(blk, s, first):
        rows = blk["rows"]
        s = s + jnp.concatenate([bias_ref[blk["edge"]]] * A_HEADS, axis=0)
        m_new = jnp.max(s, axis=-1, keepdims=True)
        alpha = None
        if not first:
            m_prev = jnp.concatenate([m_sc[h, rows, :] for h in range(A_HEADS)], axis=0)
            l_prev = jnp.concatenate([l_sc[h, rows, :] for h in range(A_HEADS)], axis=0)
            m_new = jnp.maximum(m_prev, m_new)
            alpha = jnp.exp2(m_prev - m_new)
        p = jnp.exp2(s - widen(m_new, span))
        l_new = jnp.sum(p, axis=-1, keepdims=True)
        if not first:
            l_new = l_new + alpha * l_prev
        return _bf16(p), m_new, l_new, alpha

    def update(blk, pv, m_new, l_new, alpha, first, last):
        rows = blk["rows"]
        o_new = per_head_lanes(pv)
        if not first:
            o_prev = jnp.concatenate([o_sc[s, rows, :] for s in range(o_sc.shape[0])], axis=1)
            o_new = o_new + per_head_lanes(alpha) * o_prev
        if last:
            o_new = o_new / per_head_lanes(l_new)
        else:
            for h in range(A_HEADS):
                hs = slice(h * Q_BLOCK, (h + 1) * Q_BLOCK)
                m_sc[h, rows, :] = jnp.broadcast_to(m_new[hs], (Q_BLOCK, 128))
                l_sc[h, rows, :] = jnp.broadcast_to(l_new[hs], (Q_BLOCK, 128))
        for s in range(o_sc.shape[0]):
            o_sc[s, rows, :] = o_new[:, s * 128:(s + 1) * 128]

    def process(blocks, first, last):
        ss = [scores(blk) for blk in blocks]
        res = []
        for blk, s in zip(blocks, ss):
            p, m_new, l_new, alpha = softmax(blk, s, first)
            half = p.shape[0] // 2
            pv = jnp.concatenate([_dot(p[:half], blk["vw"]), _dot(p[half:], blk["vw"])], axis=0)
            res.append((pv, m_new, l_new, alpha))
        for blk, (pv, m_new, l_new, alpha) in zip(blocks, res):
            update(blk, pv, m_new, l_new, alpha, first, last)

    for ci, (_, dil) in enumerate(A_CONFIGS):
        q_ref, k_ref, v_ref = qkv_refs[ci]
        sub_len = seq // dil
        blocks_per_res = A_STEP // (dil * Q_BLOCK)
        first, last = ci == 0, ci == n_cfg - 1

        def block(idx, dil=dil, q_ref=q_ref, k_ref=k_ref, v_ref=v_ref, sub_len=sub_len,
                  blocks_per_res=blocks_per_res):
            r, qb = idx // blocks_per_res, idx % blocks_per_res
            base = i * (A_STEP // dil) + qb * Q_BLOCK
            start = pl.multiple_of(jnp.clip(base - A_RADIUS, 0, sub_len - span), A_RADIUS)
            q_rows = pl.ds(pl.multiple_of(qb * Q_BLOCK, Q_BLOCK), Q_BLOCK)
            if dil == 1:
                q, kw, vw = q_ref[0, q_rows, :], k_ref[0, pl.ds(start, span), :], v_ref[0, pl.ds(start, span), :]
                rows = q_rows
            else:
                q, kw, vw = (q_ref[0, r, q_rows, :], k_ref[0, r, pl.ds(start, span), :],
                             v_ref[0, r, pl.ds(start, span), :])
                rows = pl.ds(qb * Q_BLOCK * dil + r, Q_BLOCK, stride=dil)
            return dict(q=q, kw=kw, vw=vw, rows=rows, edge=(base - start) // A_RADIUS)

        def pair(t, carry, block=block, first=first, last=last):
            process([block(A_INTERLEAVE * t + u) for u in range(A_INTERLEAVE)], first, last)
            return carry

        lax.fori_loop(0, A_STEP // Q_BLOCK // A_INTERLEAVE, pair, 0)

    o_ref[0] = _bf16(jnp.concatenate([o_sc[s] for s in range(o_sc.shape[0])], axis=1))


def _attn_a(a_qkv, batch, seq):
    single = pl.Buffered(1)
    args, in_specs = [], []
    for (_, dil), arr in zip(A_CONFIGS, a_qkv):
        sub_len = seq // dil
        if dil == 1:
            q_spec = pl.BlockSpec((1, A_STEP, A_WIDTH), lambda b, i: (b, i, 0))
            kv_spec = lambda part: pl.BlockSpec((1, seq, A_WIDTH), lambda b, i: (b, 0, part),
                                                pipeline_mode=single)
        else:
            q_spec = pl.BlockSpec((1, dil, A_STEP // dil, A_WIDTH), lambda b, i: (b, 0, i, 0))
            kv_spec = lambda part, dil=dil, sub_len=sub_len: pl.BlockSpec(
                (1, dil, sub_len, A_WIDTH), lambda b, i: (b, 0, 0, part), pipeline_mode=single)
        args += [arr, arr, arr]
        in_specs += [q_spec, kv_spec(1), kv_spec(2)]
    span = Q_BLOCK + 2 * A_RADIUS
    rel = np.arange(Q_BLOCK)[:, None] - np.arange(span)[None, :]
    bias = jnp.asarray(np.stack([np.where(np.abs(rel + e * A_RADIUS) <= A_RADIUS, 0.0, NEG_INF)
                                 for e in range(3)]), jnp.float32)
    hmask = jnp.asarray((np.arange(A_WIDTH)[None, :] // HEAD_DIM == np.arange(A_HEADS)[:, None])[:, None, :],
                        jnp.bfloat16)
    const3 = lambda b, i: (0, 0, 0)
    out = pl.pallas_call(
        functools.partial(_attn_a_kernel, seq=seq), grid=(batch, seq // A_STEP),
        in_specs=in_specs + [pl.BlockSpec(bias.shape, const3), pl.BlockSpec(hmask.shape, const3)],
        out_specs=pl.BlockSpec((1, A_STEP, A_WIDTH), lambda b, i: (b, i, 0)),
        out_shape=jax.ShapeDtypeStruct((batch, seq, A_WIDTH), jnp.bfloat16),
        scratch_shapes=[pltpu.VMEM((A_WIDTH // 128, A_STEP, 128), jnp.float32),
                        pltpu.VMEM((A_HEADS, A_STEP, 128), jnp.float32),
                        pltpu.VMEM((A_HEADS, A_STEP, 128), jnp.float32)],
        compiler_params=_params("parallel", "arbitrary"), name="attn_a",
    )(*args, bias, hmask)
    return out.reshape(batch * seq, A_WIDTH)


def _attn_b_kernel(qt_ref, k_ref, vt_ref, ot_ref, wq_sc, m_sc, acc_sc, st_sc, mx_sc):
    assert B_UNROLL % 2 == 0 and vt_ref.shape[1] % B_UNROLL == 0
    tq = qt_ref.shape[2]
    n_kt = vt_ref.shape[1]
    group = B_Q_HEADS // B_KV_HEADS
    wq_sc[...] = jnp.zeros(wq_sc.shape, wq_sc.dtype)
    for h in range(B_Q_HEADS):
        g, hh = divmod(h, group)
        wq_sc[g, g * HEAD_DIM:(g + 1) * HEAD_DIM, hh * tq:(hh + 1) * tq] = (
            qt_ref[0, h * HEAD_DIM:(h + 1) * HEAD_DIM, :])
    m_sc[...] = jnp.full(m_sc.shape, NEG_INF, jnp.float32)
    acc_sc[...] = jnp.zeros(acc_sc.shape, jnp.float32)

    def k_tile(j):
        return k_ref[0, pl.ds(pl.multiple_of(j * B_TILE, B_TILE), B_TILE), :]

    def softmax_pv(j, slot, g, hh):
        sl = slice(hh * tq, (hh + 1) * tq)
        vt = vt_ref[0, j, g * B_VT_ROWS:(g + 1) * B_VT_ROWS, :]
        m_prev = m_sc[g, :, sl]
        m_new = jnp.maximum(m_prev, mx_sc[slot, g, :, sl])
        alpha = jnp.exp2(m_prev - m_new)
        p = jnp.exp2(st_sc[slot, g, :, sl] - m_new)
        acc_sc[g, :, sl] = alpha * acc_sc[g, :, sl] + _dot(vt, _bf16(p))
        m_sc[g, :, sl] = m_new

    def score_tile(k, slot, g, hh):
        sl = slice(hh * tq, (hh + 1) * tq)
        st = _dot(k, wq_sc[g, :, sl])
        st_sc[slot, g, :, sl] = st
        mx_sc[slot, g, :, sl] = jnp.max(st, axis=0, keepdims=True)

    for g in range(B_KV_HEADS):
        for hh in range(group):
            score_tile(k_tile(0), 0, g, hh)

    def step(j, slot):
        k_next = k_tile(jnp.minimum(j + 1, n_kt - 1))
        for g in range(B_KV_HEADS):
            for hh in range(group):
                score_tile(k_next, 1 - slot, g, hh)
                softmax_pv(j, slot, g, hh)

    def body(jj, carry):
        for u in range(B_UNROLL):
            step(B_UNROLL * jj + u, u % 2)
        return carry

    lax.fori_loop(0, n_kt // B_UNROLL, body, 0)
    for h in range(B_Q_HEADS):
        g, hh = divmod(h, group)
        sl = slice(hh * tq, (hh + 1) * tq)
        ot_ref[0, h * HEAD_DIM:(h + 1) * HEAD_DIM, :] = _bf16(
            acc_sc[g, 0:HEAD_DIM, sl] / acc_sc[g, HEAD_DIM:HEAD_DIM + 1, sl])


def _attn_b(qbt, kb, vbt, batch, seq):
    tq = B_TILE
    group = B_Q_HEADS // B_KV_HEADS
    kb3 = kb.reshape(batch, seq, B_KV_WIDTH)
    return pl.pallas_call(
        _attn_b_kernel, grid=(batch, seq // tq),
        in_specs=[
            pl.BlockSpec((1, B_Q_WIDTH, tq), lambda b, i: (b, 0, i)),
            pl.BlockSpec((1, seq, B_KV_WIDTH), lambda b, i: (b, 0, 0)),
            pl.BlockSpec((1, seq // B_TILE, B_KV_HEADS * B_VT_ROWS, B_TILE), lambda b, i: (b, 0, 0, 0)),
        ],
        out_specs=pl.BlockSpec((1, B_Q_WIDTH, tq), lambda b, i: (b, 0, i)),
        out_shape=jax.ShapeDtypeStruct((batch, B_Q_WIDTH, seq), jnp.bfloat16),
        scratch_shapes=[
            pltpu.VMEM((B_KV_HEADS, B_KV_WIDTH, group * tq), jnp.bfloat16),
            pltpu.VMEM((B_KV_HEADS, 1, group * tq), jnp.float32),
            pltpu.VMEM((B_KV_HEADS, B_VT_ROWS, group * tq), jnp.float32),
            pltpu.VMEM((2, B_KV_HEADS, B_TILE, group * tq), jnp.float32),
            pltpu.VMEM((2, B_KV_HEADS, 1, group * tq), jnp.float32),
        ],
        compiler_params=_params("parallel", "parallel"), name="attn_b",
    )(qbt, kb3, vbt)


def _c_window_rows():
    return C_TILE_ROWS + C_ROWS_MAX - 1


def _attn_c_kernel(q_ref, k_ref, v_ref, bias_ref, o_ref, *, rows):
    t = pl.program_id(1)
    win = _c_window_rows() * GRID_W
    ws = jnp.clip(t * C_TILE_ROWS - C_ROWS_MAX // 2, 0, rows - _c_window_rows())
    start = pl.multiple_of(ws * GRID_W, GRID_W)
    kw = k_ref[0, pl.ds(start, win), :]
    vw = v_ref[0, pl.ds(start, win), :]
    q = q_ref[0]
    lane_head = lax.broadcasted_iota(jnp.int32, q.shape, 1) // HEAD_DIM
    out = jnp.zeros(q.shape, jnp.float32)
    for h in range(C_HEADS):
        qh = jnp.where(lane_head == h, q, jnp.zeros_like(q))
        s = lax.dot_general(qh, kw, _NT, preferred_element_type=jnp.float32) + bias_ref[0, h]
        m = jnp.max(s, axis=-1, keepdims=True)
        p = jnp.exp2(s - m)
        l = jnp.sum(p, axis=-1, keepdims=True)
        pv = _dot(_bf16(p), vw)
        out = jnp.where(lane_head == h, pv / l, out)
    o_ref[0] = _bf16(out)


def _c_bias_tables(rpb, rows):
    n_tiles = rows // C_TILE_ROWS
    wr = _c_window_rows()
    kr = min(C_ROWS_MAX, rows)
    c = np.arange(GRID_W)
    c0 = np.clip(c - C_COLS // 2, 0, GRID_W - C_COLS)
    col_ok = (c[None, :] >= c0[:, None]) & (c[None, :] < c0[:, None] + C_COLS)
    pad = GRID_W - C_COLS
    padded = jnp.pad(rpb.astype(jnp.float32) * LOG2E, ((0, 0), (0, 0), (0, 0), (pad, pad)))
    shifted = jnp.stack([padded[..., GRID_W - 1 - qc:2 * GRID_W - 1 - qc] for qc in range(GRID_W)],
                        axis=-2)
    by_dr = jnp.where(col_ok, shifted, NEG_INF)
    masked = jnp.full(by_dr.shape[:2] + (GRID_W, GRID_W), NEG_INF, jnp.float32)
    tabs = []
    for t in (0, 1, n_tiles - 1):
        ws = int(np.clip(t * C_TILE_ROWS - C_ROWS_MAX // 2, 0, rows - wr))
        q_blocks = []
        for r in range(t * C_TILE_ROWS, (t + 1) * C_TILE_ROWS):
            r0 = int(np.clip(r - kr // 2, 0, rows - kr))
            k_blocks = [by_dr[:, :, krow - r + C_ROWS_MAX - 1] if r0 <= krow < r0 + kr else masked
                        for krow in range(ws, ws + wr)]
            q_blocks.append(jnp.concatenate(k_blocks, axis=-1))
        tabs.append(jnp.concatenate(q_blocks, axis=-2))
    return jnp.stack(tabs, axis=1)


def _attn_c(qc, kc, vc, bias_tabs, layer, batch, seq):
    rows = seq // GRID_W
    n_tiles = rows // C_TILE_ROWS
    tq = C_TILE_ROWS * GRID_W
    view = lambda t: t.reshape(batch, seq, C_WIDTH)
    cls = lambda b, t: (layer, jnp.where(t == 0, 0, jnp.where(t == n_tiles - 1, 2, 1)), 0, 0, 0)
    out = pl.pallas_call(
        functools.partial(_attn_c_kernel, rows=rows), grid=(batch, n_tiles),
        in_specs=[
            pl.BlockSpec((1, tq, C_WIDTH), lambda b, t: (b, t, 0)),
            pl.BlockSpec((1, seq, C_WIDTH), lambda b, t: (b, 0, 0)),
            pl.BlockSpec((1, seq, C_WIDTH), lambda b, t: (b, 0, 0)),
            pl.BlockSpec((None, 1) + bias_tabs.shape[2:], cls),
        ],
        out_specs=pl.BlockSpec((1, tq, C_WIDTH), lambda b, t: (b, t, 0)),
        out_shape=jax.ShapeDtypeStruct((batch, seq, C_WIDTH), jnp.bfloat16),
        compiler_params=_params("parallel", "arbitrary"), name="attn_c",
    )(view(qc), view(kc), view(vc), bias_tabs)
    return out.reshape(batch * seq, C_WIDTH)


def _mix_kernel(x_ref, oa_ref, obt_ref, oc_ref, wg_ref, bg_ref, wa_ref, wb_ref, wc_ref, wo_ref,
                g_ref, b_ref, y_ref, *, alpha):
    x = x_ref[...]
    xb = _bf16(x)
    d = x.shape[1]
    projs = (
        _dot(oa_ref[...], wa_ref[...]),
        lax.dot_general(obt_ref[0], wb_ref[...], _TN, preferred_element_type=jnp.float32),
        _dot(oc_ref[...], wc_ref[...]),
    )
    merged = None
    for br, proj in enumerate(projs):
        cols = slice(QKV_COLS + br * d, QKV_COLS + (br + 1) * d)
        logits = _dot(xb, wg_ref[:, cols]) + bg_ref[:, br * d:(br + 1) * d]
        term = jax.nn.sigmoid(logits) * proj
        merged = term if merged is None else merged + term
    mix = _dot(_bf16(merged), wo_ref[...])
    y_ref[...] = _layer_norm(alpha * x + mix, g_ref[...], b_ref[...])


def _mix(x2, oa, obt, oc, w_in, bg, wa, wb, wc, wo, g, b, layer, alpha, seq):
    n, d = x2.shape
    tm = ROW_TILE
    nt = seq // tm
    row = lambda i: (i, 0)
    const = lambda i: (0, 0)
    full = lambda a: pl.BlockSpec(a.shape, const)
    return pl.pallas_call(
        functools.partial(_mix_kernel, alpha=alpha), grid=(n // tm,),
        in_specs=[
            pl.BlockSpec((tm, d), row), pl.BlockSpec((tm, A_WIDTH), row),
            pl.BlockSpec((1, B_Q_WIDTH, tm), lambda i: (i // nt, 0, i % nt)),
            pl.BlockSpec((tm, C_WIDTH), row),
            _layer_spec(w_in, layer), full(bg), _layer_spec(wa, layer), _layer_spec(wb, layer),
            _layer_spec(wc, layer), _layer_spec(wo, layer), full(g), full(b),
        ],
        out_specs=pl.BlockSpec((tm, d), row),
        out_shape=jax.ShapeDtypeStruct((n, d), jnp.float32),
        compiler_params=_params("parallel"), name="mix",
    )(x2, oa, obt, oc, w_in, bg, wa, wb, wc, wo, g, b)


def _mlp_kernel(x_ref, wu_ref, wd_ref, g_ref, b_ref, y_ref, *, alpha, chunk):
    x = x_ref[...]
    xb = _bf16(x)
    ff = None
    for c in range(wu_ref.shape[1] // chunk):
        hid = jnp.square(jnp.maximum(_dot(xb, wu_ref[:, c * chunk:(c + 1) * chunk]), 0.0))
        part = _dot(_bf16(hid), wd_ref[c * chunk:(c + 1) * chunk, :])
        ff = part if ff is None else ff + part
    y_ref[...] = _layer_norm(alpha * x + ff, g_ref[...], b_ref[...])


def _mlp(x2, wu, wd, g, b, layer, alpha):
    n, d = x2.shape
    tm = ROW_TILE
    row = lambda i: (i, 0)
    const = lambda i: (0, 0)
    return pl.pallas_call(
        functools.partial(_mlp_kernel, alpha=alpha, chunk=1024), grid=(n // tm,),
        in_specs=[
            pl.BlockSpec((tm, d), row),
            _layer_spec(wu, layer), _layer_spec(wd, layer),
            pl.BlockSpec(g.shape, const), pl.BlockSpec(b.shape, const),
        ],
        out_specs=pl.BlockSpec((tm, d), row),
        out_shape=jax.ShapeDtypeStruct((n, d), jnp.float32),
        compiler_params=_params("parallel"), name="mlp",
    )(x2, wu, wd, g, b)


def _rotary_tables(pos_list, theta_list, half, width_per_head, reps):
    s = pos_list[0].shape[0]
    c = jnp.ones((s, width_per_head), jnp.float32)
    sa = jnp.zeros((s, width_per_head), jnp.float32)
    sb = jnp.zeros((s, width_per_head), jnp.float32)
    for i, (pos, theta) in enumerate(zip(pos_list, theta_list)):
        inv = theta ** (-jnp.arange(half, dtype=jnp.float32) / half)
        ang = pos.astype(jnp.float32)[:, None] * inv[None, :]
        cos, sin = jnp.cos(ang), jnp.sin(ang)
        o = 2 * half * i
        c = c.at[:, o:o + half].set(cos).at[:, o + half:o + 2 * half].set(cos)
        sa = sa.at[:, o:o + half].set(-sin)
        sb = sb.at[:, o + half:o + 2 * half].set(sin)
    tile = lambda t: jnp.tile(t, (1, reps))
    return tile(c), tile(sa), tile(sb)


def kernel(x, w_in, b_gate, q_norm_b, k_norm_b, rpb_c, w_branch_a, w_branch_b, w_branch_c, w_out,
           ln1_g, ln1_b, w_up, w_down, ln2_g, ln2_b):
    batch, seq, d = x.shape
    depth = w_in.shape[0]
    alpha = (2 * depth) ** 0.25
    rows = seq // GRID_W
    assert seq % (ROW_TILE) == 0 and rows % C_TILE_ROWS == 0 and rows >= 2 * _c_window_rows()
    assert all(window // (2 * dil) == A_RADIUS and (seq // dil) >= Q_BLOCK + 2 * A_RADIUS
               for window, dil in A_CONFIGS)
    assert A_CONFIGS[0][1] == 1 and seq % A_STEP == 0 and len(A_CONFIGS) == 3

    pos = jnp.arange(seq)
    tabs_a = _rotary_tables([pos], [A_ROPE_THETA], A_ROPE_DIMS // 2, HEAD_DIM, A_HEADS)
    tabs_b = _rotary_tables([pos // GRID_W, pos % GRID_W], [B_AXIAL_THETA, B_AXIAL_THETA],
                            HEAD_DIM // 4, HEAD_DIM, B_KV_HEADS)
    head_of = np.arange(B_Q_WIDTH) // HEAD_DIM
    ones_blk = jnp.asarray(head_of[:, None] == head_of[None, :], jnp.bfloat16)

    bf = jnp.bfloat16
    w_in, w_branch_a, w_branch_b, w_branch_c, w_out, w_up, w_down = (
        w.astype(bf) for w in (w_in, w_branch_a, w_branch_b, w_branch_c, w_out, w_up, w_down))
    c_bias = _c_bias_tables(rpb_c, rows)
    x2 = x.reshape(batch * seq, d)
    for layer in range(depth):
        qg = jnp.tile(q_norm_b[layer], B_Q_HEADS)[None, :]
        kg = jnp.tile(k_norm_b[layer], B_KV_HEADS)[None, :]
        a1, a4, a16, qbt, kb, vbt, qc, kc, vc = _in_proj(
            x2, w_in, layer, tabs_a, tabs_b, qg, kg, ones_blk, batch, seq)
        oa = _attn_a((a1, a4, a16), batch, seq)
        obt = _attn_b(qbt, kb, vbt, batch, seq)
        oc = _attn_c(qc, kc, vc, c_bias, layer, batch, seq)
        x2 = _mix(x2, oa, obt, oc, w_in, b_gate[layer][None, :],
                  w_branch_a, w_branch_b, w_branch_c, w_out,
                  ln1_g[layer][None, :], ln1_b[layer][None, :], layer, alpha, seq)
        x2 = _mlp(x2, w_up, w_down, ln2_g[layer][None, :], ln2_b[layer][None, :], layer, alpha)
    return x2.reshape(batch, seq, d)
```

```python
import functools
import math

import numpy as np
import jax
import jax.numpy as jnp
from jax import lax
from jax.experimental import pallas as pl
from jax.experimental.pallas import tpu as pltpu

HEAD_DIM = 64
A_HEADS = 4
A_CONFIGS = ((128, 1), (512, 4), (2048, 16))
A_ROPE_DIMS = HEAD_DIM // 4
A_ROPE_THETA = 500000.0
B_Q_HEADS = 8
B_KV_HEADS = 2
B_AXIAL_THETA = 10000.0
C_HEADS = 4
C_ROWS_MAX = 8
C_COLS = 16
GRID_W = 64
Q_BLOCK = 128
LN_EPS = 1e-5
RMS_EPS = 1e-6
NEG_INF = -1e30
LOG2E = math.log2(math.e)

A_WIDTH = A_HEADS * HEAD_DIM
B_Q_WIDTH = B_Q_HEADS * HEAD_DIM
B_KV_WIDTH = B_KV_HEADS * HEAD_DIM
C_WIDTH = C_HEADS * HEAD_DIM
QKV_COLS = 3 * A_WIDTH + B_Q_WIDTH + 2 * B_KV_WIDTH + 3 * C_WIDTH

VMEM_LIMIT_BYTES = 56 * 1024 * 1024
ROW_TILE = 1024
B_TILE = 256
B_VT_ROWS = HEAD_DIM + 16
B_UNROLL = 16
C_TILE_ROWS = 4
A_RADIUS = A_CONFIGS[0][0] // (2 * A_CONFIGS[0][1])
A_STEP = Q_BLOCK * max(dil for _, dil in A_CONFIGS)
A_INTERLEAVE = 4

_NT = (((1,), (1,)), ((), ()))
_TN = (((0,), (0,)), ((), ()))


def _params(*sem):
    return pltpu.CompilerParams(dimension_semantics=sem, vmem_limit_bytes=VMEM_LIMIT_BYTES)


def _bf16(x):
    return x.astype(jnp.bfloat16)


def _dot(a, b):
    return jnp.dot(a, b, preferred_element_type=jnp.float32)


def _layer_norm(y, g, b):
    mu = jnp.mean(y, axis=-1, keepdims=True)
    d = y - mu
    var = jnp.mean(d * d, axis=-1, keepdims=True)
    return d * lax.rsqrt(var + LN_EPS) * g + b


def _rotate(x, c, sa, sb, half):
    w = x.shape[-1]
    return x * c + pltpu.roll(x, w - half, 1) * sa + pltpu.roll(x, half, 1) * sb


def _head_mean_sq(h, ones_blk):
    sq = h * h
    hi = _bf16(sq)
    lo = _bf16(sq - hi.astype(jnp.float32))
    return (_dot(hi, ones_blk) + _dot(lo, ones_blk)) * (1.0 / HEAD_DIM)


def _in_proj_kernel(x_ref, w_ref, ca_ref, saa_ref, sba_ref, cb_ref, sab_ref, sbb_ref,
                    qg_ref, kg_ref, ones_ref,
                    a1_ref, a4_ref, a16_ref, qbt_ref, kb_ref, vbt_ref, qc_ref, kc_ref, vc_ref,
                    stage_sc):
    xb = _bf16(x_ref[...])
    scale = HEAD_DIM ** -0.5

    off_b = 3 * A_WIDTH
    off_c = off_b + B_Q_WIDTH + 2 * B_KV_WIDTH

    ha = _dot(xb, w_ref[:, 0:off_b])
    ca, saa, sba = ca_ref[...], saa_ref[...], sba_ref[...]
    half_a = A_ROPE_DIMS // 2
    qa = _rotate(ha[:, 0:A_WIDTH], ca, saa, sba, half_a) * (scale * LOG2E)
    ka = _rotate(ha[:, A_WIDTH:2 * A_WIDTH], ca, saa, sba, half_a)
    qkv_a = (qa, ka, ha[:, 2 * A_WIDTH:3 * A_WIDTH])
    for part, val in enumerate(qkv_a):
        a1_ref[0, :, part * A_WIDTH:(part + 1) * A_WIDTH] = _bf16(val)
        for s in range(A_WIDTH // 128):
            stage_sc[part * (A_WIDTH // 128) + s] = val[:, s * 128:(s + 1) * 128]
    tm = x_ref.shape[0]
    for out_ref in (a4_ref, a16_ref):
        dil = out_ref.shape[1]
        for r in range(dil):
            for s in range(stage_sc.shape[0]):
                out_ref[0, r, :, s * 128:(s + 1) * 128] = _bf16(
                    stage_sc[s, pl.ds(r, tm // dil, stride=dil), :])

    hb = _dot(xb, w_ref[:, off_b:off_c])
    ones_q = ones_ref[...]
    ones_k = ones_ref[0:B_KV_WIDTH, 0:B_KV_WIDTH]
    cb, sab, sbb = cb_ref[...], sab_ref[...], sbb_ref[...]
    rep = B_Q_WIDTH // B_KV_WIDTH
    cb_q = jnp.concatenate([cb] * rep, axis=1)
    sab_q = jnp.concatenate([sab] * rep, axis=1)
    sbb_q = jnp.concatenate([sbb] * rep, axis=1)
    half_b = HEAD_DIM // 4
    qb = hb[:, 0:B_Q_WIDTH]
    qb = qb * lax.rsqrt(_head_mean_sq(qb, ones_q) + RMS_EPS) * qg_ref[...]
    qb = _rotate(qb, cb_q, sab_q, sbb_q, half_b) * (scale * LOG2E)
    kb = hb[:, B_Q_WIDTH:B_Q_WIDTH + B_KV_WIDTH]
    kb = kb * lax.rsqrt(_head_mean_sq(kb, ones_k) + RMS_EPS) * kg_ref[...]
    kb = _rotate(kb, cb, sab, sbb, half_b)
    vb = hb[:, B_Q_WIDTH + B_KV_WIDTH:]
    qbt_ref[0] = _bf16(qb.T)
    kb_ref[...] = _bf16(kb)
    ones_rows = jnp.ones((B_VT_ROWS - HEAD_DIM, B_TILE), jnp.bfloat16)
    for c in range(vbt_ref.shape[1]):
        vt = _bf16(vb[c * B_TILE:(c + 1) * B_TILE, :].T)
        for g in range(B_KV_HEADS):
            vbt_ref[0, c, g * B_VT_ROWS:g * B_VT_ROWS + HEAD_DIM, :] = vt[g * HEAD_DIM:(g + 1) * HEAD_DIM]
            vbt_ref[0, c, g * B_VT_ROWS + HEAD_DIM:(g + 1) * B_VT_ROWS, :] = ones_rows

    hc = _dot(xb, w_ref[:, off_c:off_c + 3 * C_WIDTH])
    qc_ref[...] = _bf16(hc[:, 0:C_WIDTH] * (scale * LOG2E))
    kc_ref[...] = _bf16(hc[:, C_WIDTH:2 * C_WIDTH])
    vc_ref[...] = _bf16(hc[:, 2 * C_WIDTH:3 * C_WIDTH])


def _layer_spec(stacked, layer):
    zeros = (0,) * (stacked.ndim - 1)
    return pl.BlockSpec((None,) + stacked.shape[1:], lambda *_: (layer,) + zeros,
                        pipeline_mode=pl.Buffered(1))


def _in_proj(x2, w_in, layer, tabs_a, tabs_b, qg, kg, ones_blk, batch, seq):
    n, d = x2.shape
    tm = ROW_TILE
    nt = seq // tm
    row = lambda i: (i, 0)
    pos = lambda i: (i % nt, 0)
    const = lambda i: (0, 0)
    bf = jnp.bfloat16
    out_shape = (
        jax.ShapeDtypeStruct((batch, seq, 3 * A_WIDTH), bf),
        *(jax.ShapeDtypeStruct((batch, dil, seq // dil, 3 * A_WIDTH), bf) for _, dil in A_CONFIGS[1:]),
        jax.ShapeDtypeStruct((batch, B_Q_WIDTH, seq), bf),
        jax.ShapeDtypeStruct((n, B_KV_WIDTH), bf),
        jax.ShapeDtypeStruct((batch, seq // B_TILE, B_KV_HEADS * B_VT_ROWS, B_TILE), bf),
        jax.ShapeDtypeStruct((n, C_WIDTH), bf), jax.ShapeDtypeStruct((n, C_WIDTH), bf),
        jax.ShapeDtypeStruct((n, C_WIDTH), bf),
    )
    out_specs = (
        pl.BlockSpec((1, tm, 3 * A_WIDTH), lambda i: (i // nt, i % nt, 0)),
        *(pl.BlockSpec((1, dil, tm // dil, 3 * A_WIDTH), lambda i: (i // nt, 0, i % nt, 0))
          for _, dil in A_CONFIGS[1:]),
        pl.BlockSpec((1, B_Q_WIDTH, tm), lambda i: (i // nt, 0, i % nt)),
        pl.BlockSpec((tm, B_KV_WIDTH), row),
        pl.BlockSpec((1, tm // B_TILE, B_KV_HEADS * B_VT_ROWS, B_TILE),
                     lambda i: (i // nt, i % nt, 0, 0)),
        pl.BlockSpec((tm, C_WIDTH), row), pl.BlockSpec((tm, C_WIDTH), row),
        pl.BlockSpec((tm, C_WIDTH), row),
    )
    in_specs = [
        pl.BlockSpec((tm, d), row),
        _layer_spec(w_in, layer),
        pl.BlockSpec((tm, A_WIDTH), pos), pl.BlockSpec((tm, A_WIDTH), pos),
        pl.BlockSpec((tm, A_WIDTH), pos),
        pl.BlockSpec((tm, B_KV_WIDTH), pos), pl.BlockSpec((tm, B_KV_WIDTH), pos),
        pl.BlockSpec((tm, B_KV_WIDTH), pos),
        pl.BlockSpec(qg.shape, const), pl.BlockSpec(kg.shape, const),
        pl.BlockSpec(ones_blk.shape, const),
    ]
    return pl.pallas_call(
        _in_proj_kernel, grid=(n // tm,), in_specs=in_specs, out_specs=out_specs,
        out_shape=out_shape,
        scratch_shapes=[pltpu.VMEM((3 * A_WIDTH // 128, tm, 128), jnp.float32)],
        compiler_params=_params("parallel"), name="in_proj",
    )(x2, w_in, *tabs_a, *tabs_b, qg, kg, ones_blk)


def _attn_a_kernel(*refs, seq):
    n_cfg = len(A_CONFIGS)
    qkv_refs = [refs[3 * c:3 * c + 3] for c in range(n_cfg)]
    bias_ref, hmask_ref, o_ref, o_sc, m_sc, l_sc = refs[3 * n_cfg:]
    i = pl.program_id(1)
    span = Q_BLOCK + 2 * A_RADIUS
    lane_head = lax.broadcasted_iota(jnp.int32, (Q_BLOCK, A_WIDTH), 1) // HEAD_DIM

    def widen(t, width):
        return t if t.shape[1] in (1, width) else jnp.concatenate([t] * (width // t.shape[1]), axis=1)

    def per_head_lanes(stacked):
        tiles = [widen(stacked[h * Q_BLOCK:(h + 1) * Q_BLOCK], A_WIDTH) for h in range(A_HEADS)]
        out = tiles[-1]
        for h in range(A_HEADS - 2, -1, -1):
            out = jnp.where(lane_head == h, tiles[h], out)
        return out

    def scores(blk):
        halves = []
        for h0 in range(0, A_HEADS, A_HEADS // 2):
            qm = jnp.concatenate([blk["q"] * hmask_ref[h] for h in range(h0, h0 + A_HEADS // 2)], axis=0)
            halves.append(lax.dot_general(qm, blk["kw"], _NT, preferred_element_type=jnp.float32))
        return jnp.concatenate(halves, axis=0)

    def softmax(blk, s, first):
        rows = blk["rows"]
        s = s + jnp.concatenate([bias_ref[blk["edge"]]] * A_HEADS, axis=0)
        m_new = jnp.max(s, axis=-1, keepdims=True)
        alpha = None
        if not first:
            m_prev = jnp.concatenate([m_sc[h, rows, :] for h in range(A_HEADS)], axis=0)
            l_prev = jnp.concatenate([l_sc[h, rows, :] for h in range(A_HEADS)], axis=0)
            m_new = jnp.maximum(m_prev, m_new)
            alpha = jnp.exp2(m_prev - m_new)
        p = jnp.exp2(s - widen(m_new, span))
        l_new = jnp.sum(p, axis=-1, keepdims=True)
        if not first:
            l_new = l_new + alpha * l_prev
        return _bf16(p), m_new, l_new, alpha

    def update(blk, pv, m_new, l_new, alpha, first, last):
        rows = blk["rows"]
        o_new = per_head_lanes(pv)
        if not first:
            o_prev = jnp.concatenate([o_sc[s, rows, :] for s in range(o_sc.shape[0])], axis=1)
            o_new = o_new + per_head_lanes(alpha) * o_prev
        if last:
            o_new = o_new / per_head_lanes(l_new)
        else:
            for h in range(A_HEADS):
                hs = slice(h * Q_BLOCK, (h + 1) * Q_BLOCK)
                m_sc[h, rows, :] = jnp.broadcast_to(m_new[hs], (Q_BLOCK, 128))
                l_sc[h, rows, :] = jnp.broadcast_to(l_new[hs], (Q_BLOCK, 128))
        for s in range(o_sc.shape[0]):
            o_sc[s, rows, :] = o_new[:, s * 128:(s + 1) * 128]

    def process(blocks, first, last):
        ss = [scores(blk) for blk in blocks]
        res = []
        for blk, s in zip(blocks, ss):
            p, m_new, l_new, alpha = softmax(blk, s, first)
            half = p.shape[0] // 2
            pv = jnp.concatenate([_dot(p[:half], blk["vw"]), _dot(p[half:], blk["vw"])], axis=0)
            res.append((pv, m_new, l_new, alpha))
        for blk, (pv, m_new, l_new, alpha) in zip(blocks, res):
            update(blk, pv, m_new, l_new, alpha, first, last)

    for ci, (_, dil) in enumerate(A_CONFIGS):
        q_ref, k_ref, v_ref = qkv_refs[ci]
        sub_len = seq // dil
        blocks_per_res = A_STEP // (dil * Q_BLOCK)
        first, last = ci == 0, ci == n_cfg - 1

        def block(idx, dil=dil, q_ref=q_ref, k_ref=k_ref, v_ref=v_ref, sub_len=sub_len,
                  blocks_per_res=blocks_per_res):
            r, qb = idx // blocks_per_res, idx % blocks_per_res
            base = i * (A_STEP // dil) + qb * Q_BLOCK
            start = pl.multiple_of(jnp.clip(base - A_RADIUS, 0, sub_len - span), A_RADIUS)
            q_rows = pl.ds(pl.multiple_of(qb * Q_BLOCK, Q_BLOCK), Q_BLOCK)
            if dil == 1:
                q, kw, vw = q_ref[0, q_rows, :], k_ref[0, pl.ds(start, span), :], v_ref[0, pl.ds(start, span), :]
                rows = q_rows
            else:
                q, kw, vw = (q_ref[0, r, q_rows, :], k_ref[0, r, pl.ds(start, span), :],
                             v_ref[0, r, pl.ds(start, span), :])
                rows = pl.ds(qb * Q_BLOCK * dil + r, Q_BLOCK, stride=dil)
            return dict(q=q, kw=kw, vw=vw, rows=rows, edge=(base - start) // A_RADIUS)

        def pair(t, carry, block=block, first=first, last=last):
            process([block(A_INTERLEAVE * t + u) for u in range(A_INTERLEAVE)], first, last)
            return carry

        lax.fori_loop(0, A_STEP // Q_BLOCK // A_INTERLEAVE, pair, 0)

    o_ref[0] = _bf16(jnp.concatenate([o_sc[s] for s in range(o_sc.shape[0])], axis=1))


def _attn_a(a_qkv, batch, seq):
    single = pl.Buffered(1)
    args, in_specs = [], []
    for (_, dil), arr in zip(A_CONFIGS, a_qkv):
        sub_len = seq // dil
        if dil == 1:
            q_spec = pl.BlockSpec((1, A_STEP, A_WIDTH), lambda b, i: (b, i, 0))
            kv_spec = lambda part: pl.BlockSpec((1, seq, A_WIDTH), lambda b, i: (b, 0, part),
                                                pipeline_mode=single)
        else:
            q_spec = pl.BlockSpec((1, dil, A_STEP // dil, A_WIDTH), lambda b, i: (b, 0, i, 0))
            kv_spec = lambda part, dil=dil, sub_len=sub_len: pl.BlockSpec(
                (1, dil, sub_len, A_WIDTH), lambda b, i: (b, 0, 0, part), pipeline_mode=single)
        args += [arr, arr, arr]
        in_specs += [q_spec, kv_spec(1), kv_spec(2)]
    span = Q_BLOCK + 2 * A_RADIUS
    rel = np.arange(Q_BLOCK)[:, None] - np.arange(span)[None, :]
    bias = jnp.asarray(np.stack([np.where(np.abs(rel + e * A_RADIUS) <= A_RADIUS, 0.0, NEG_INF)
                                 for e in range(3)]), jnp.float32)
    hmask = jnp.asarray((np.arange(A_WIDTH)[None, :] // HEAD_DIM == np.arange(A_HEADS)[:, None])[:, None, :],
                        jnp.bfloat16)
    const3 = lambda b, i: (0, 0, 0)
    out = pl.pallas_call(
        functools.partial(_attn_a_kernel, seq=seq), grid=(batch, seq // A_STEP),
        in_specs=in_specs + [pl.BlockSpec(bias.shape, const3), pl.BlockSpec(hmask.shape, const3)],
        out_specs=pl.BlockSpec((1, A_STEP, A_WIDTH), lambda b, i: (b, i, 0)),
        out_shape=jax.ShapeDtypeStruct((batch, seq, A_WIDTH), jnp.bfloat16),
        scratch_shapes=[pltpu.VMEM((A_WIDTH // 128, A_STEP, 128), jnp.float32),
                        pltpu.VMEM((A_HEADS, A_STEP, 128), jnp.float32),
                        pltpu.VMEM((A_HEADS, A_STEP, 128), jnp.float32)],
        compiler_params=_params("parallel", "arbitrary"), name="attn_a",
    )(*args, bias, hmask)
    return out.reshape(batch * seq, A_WIDTH)


def _attn_b_kernel(qt_ref, k_ref, vt_ref, ot_ref, wq_sc, m_sc, acc_sc, st_sc, mx_sc):
    assert B_UNROLL % 2 == 0 and vt_ref.shape[1] % B_UNROLL == 0
    n_kt = vt_ref.shape[1]
    group = B_Q_HEADS // B_KV_HEADS
    wq_sc[...] = jnp.zeros(wq_sc.shape, wq_sc.dtype)
    for h in range(B_Q_HEADS):
        g = h // group
        wq_sc[h, g * HEAD_DIM:(g + 1) * HEAD_DIM, :] = qt_ref[0, h * HEAD_DIM:(h + 1) * HEAD_DIM, :]
    m_sc[...] = jnp.full(m_sc.shape, NEG_INF, jnp.float32)
    acc_sc[...] = jnp.zeros(acc_sc.shape, jnp.float32)

    def k_tile(j):
        return k_ref[0, pl.ds(pl.multiple_of(j * B_TILE, B_TILE), B_TILE), :]

    def softmax_pv(j, slot, h):
        g = h // group
        vt = vt_ref[0, j, g * B_VT_ROWS:(g + 1) * B_VT_ROWS, :]
        m_prev = m_sc[h]
        m_new = jnp.maximum(m_prev, mx_sc[slot, h])
        alpha = jnp.exp2(m_prev - m_new)
        p = jnp.exp2(st_sc[slot, h] - m_new)
        acc_sc[h] = alpha * acc_sc[h] + _dot(vt, _bf16(p))
        m_sc[h] = m_new

    def score_tile(k, slot, h):
        st = _dot(k, wq_sc[h])
        st_sc[slot, h] = st
        mx_sc[slot, h] = jnp.max(st, axis=0, keepdims=True)

    for h in range(B_Q_HEADS):
        score_tile(k_tile(0), 0, h)

    def step(j, slot):
        k_next = k_tile(jnp.minimum(j + 1, n_kt - 1))
        for h in range(B_Q_HEADS):
            score_tile(k_next, 1 - slot, h)
            softmax_pv(j, slot, h)

    def body(jj, carry):
        for u in range(B_UNROLL):
            step(B_UNROLL * jj + u, u % 2)
        return carry

    lax.fori_loop(0, n_kt // B_UNROLL, body, 0)
    for h in range(B_Q_HEADS):
        ot_ref[0, h * HEAD_DIM:(h + 1) * HEAD_DIM, :] = _bf16(
            acc_sc[h, 0:HEAD_DIM] / acc_sc[h, HEAD_DIM:HEAD_DIM + 1])


def _attn_b(qbt, kb, vbt, batch, seq):
    tq = B_TILE
    group = B_Q_HEADS // B_KV_HEADS
    kb3 = kb.reshape(batch, seq, B_KV_WIDTH)
    return pl.pallas_call(
        _attn_b_kernel, grid=(batch, seq // tq),
        in_specs=[
            pl.BlockSpec((1, B_Q_WIDTH, tq), lambda b, i: (b, 0, i)),
            pl.BlockSpec((1, seq, B_KV_WIDTH), lambda b, i: (b, 0, 0)),
            pl.BlockSpec((1, seq // B_TILE, B_KV_HEADS * B_VT_ROWS, B_TILE), lambda b, i: (b, 0, 0, 0)),
        ],
        out_specs=pl.BlockSpec((1, B_Q_WIDTH, tq), lambda b, i: (b, 0, i)),
        out_shape=jax.ShapeDtypeStruct((batch, B_Q_WIDTH, seq), jnp.bfloat16),
        scratch_shapes=[
            pltpu.VMEM((B_Q_HEADS, B_KV_WIDTH, tq), jnp.bfloat16),
            pltpu.VMEM((B_Q_HEADS, 1, tq), jnp.float32),
            pltpu.VMEM((B_Q_HEADS, B_VT_ROWS, tq), jnp.float32),
            pltpu.VMEM((2, B_Q_HEADS, B_TILE, tq), jnp.float32),
            pltpu.VMEM((2, B_Q_HEADS, 1, tq), jnp.float32),
        ],
        compiler_params=_params("parallel", "parallel"), name="attn_b",
    )(qbt, kb3, vbt)


def _c_window_rows():
    return C_TILE_ROWS + C_ROWS_MAX - 1


def _attn_c_kernel(q_ref, k_ref, v_ref, bias_ref, o_ref, *, rows):
    t = pl.program_id(1)
    win = _c_window_rows() * GRID_W
    ws = jnp.clip(t * C_TILE_ROWS - C_ROWS_MAX // 2, 0, rows - _c_window_rows())
    start = pl.multiple_of(ws * GRID_W, GRID_W)
    kw = k_ref[0, pl.ds(start, win), :]
    vw = v_ref[0, pl.ds(start, win), :]
    q = q_ref[0]
    lane_head = lax.broadcasted_iota(jnp.int32, q.shape, 1) // HEAD_DIM
    out = jnp.zeros(q.shape, jnp.float32)
    for h in range(C_HEADS):
        qh = jnp.where(lane_head == h, q, jnp.zeros_like(q))
        s = lax.dot_general(qh, kw, _NT, preferred_element_type=jnp.float32) + bias_ref[0, h]
        m = jnp.max(s, axis=-1, keepdims=True)
        p = jnp.exp2(s - m)
        l = jnp.sum(p, axis=-1, keepdims=True)
        pv = _dot(_bf16(p), vw)
        out = jnp.where(lane_head == h, pv / l, out)
    o_ref[0] = _bf16(out)


def _c_bias_tables(rpb, rows):
    n_tiles = rows // C_TILE_ROWS
    wr = _c_window_rows()
    kr = min(C_ROWS_MAX, rows)
    c = np.arange(GRID_W)
    c0 = np.clip(c - C_COLS // 2, 0, GRID_W - C_COLS)
    col_ok = (c[None, :] >= c0[:, None]) & (c[None, :] < c0[:, None] + C_COLS)
    pad = GRID_W - C_COLS
    padded = jnp.pad(rpb.astype(jnp.float32) * LOG2E, ((0, 0), (0, 0), (0, 0), (pad, pad)))
    shifted = jnp.stack([padded[..., GRID_W - 1 - qc:2 * GRID_W - 1 - qc] for qc in range(GRID_W)],
                        axis=-2)
    by_dr = jnp.where(col_ok, shifted, NEG_INF)
    masked = jnp.full(by_dr.shape[:2] + (GRID_W, GRID_W), NEG_INF, jnp.float32)
    tabs = []
    for t in (0, 1, n_tiles - 1):
        ws = int(np.clip(t * C_TILE_ROWS - C_ROWS_MAX // 2, 0, rows - wr))
        q_blocks = []
        for r in range(t * C_TILE_ROWS, (t + 1) * C_TILE_ROWS):
            r0 = int(np.clip(r - kr // 2, 0, rows - kr))
            k_blocks = [by_dr[:, :, krow - r + C_ROWS_MAX - 1] if r0 <= krow < r0 + kr else masked
                        for krow in range(ws, ws + wr)]
            q_blocks.append(jnp.concatenate(k_blocks, axis=-1))
        tabs.append(jnp.concatenate(q_blocks, axis=-2))
    return jnp.stack(tabs, axis=1)


def _attn_c(qc, kc, vc, bias_tabs, layer, batch, seq):
    rows = seq // GRID_W
    n_tiles = rows // C_TILE_ROWS
    tq = C_TILE_ROWS * GRID_W
    view = lambda t: t.reshape(batch, seq, C_WIDTH)
    cls = lambda b, t: (layer, jnp.where(t == 0, 0, jnp.where(t == n_tiles - 1, 2, 1)), 0, 0, 0)
    out = pl.pallas_call(
        functools.partial(_attn_c_kernel, rows=rows), grid=(batch, n_tiles),
        in_specs=[
            pl.BlockSpec((1, tq, C_WIDTH), lambda b, t: (b, t, 0)),
            pl.BlockSpec((1, seq, C_WIDTH), lambda b, t: (b, 0, 0)),
            pl.BlockSpec((1, seq, C_WIDTH), lambda b, t: (b, 0, 0)),
            pl.BlockSpec((None, 1) + bias_tabs.shape[2:], cls),
        ],
        out_specs=pl.BlockSpec((1, tq, C_WIDTH), lambda b, t: (b, t, 0)),
        out_shape=jax.ShapeDtypeStruct((batch, seq, C_WIDTH), jnp.bfloat16),
        compiler_params=_params("parallel", "arbitrary"), name="attn_c",
    )(view(qc), view(kc), view(vc), bias_tabs)
    return out.reshape(batch * seq, C_WIDTH)


def _mix_kernel(x_ref, oa_ref, obt_ref, oc_ref, wg_ref, bg_ref, wa_ref, wb_ref, wc_ref, wo_ref,
                g_ref, b_ref, y_ref, *, alpha):
    x = x_ref[...]
    xb = _bf16(x)
    d = x.shape[1]
    projs = (
        _dot(oa_ref[...], wa_ref[...]),
        lax.dot_general(obt_ref[0], wb_ref[...], _TN, preferred_element_type=jnp.float32),
        _dot(oc_ref[...], wc_ref[...]),
    )
    merged = None
    for br, proj in enumerate(projs):
        cols = slice(QKV_COLS + br * d, QKV_COLS + (br + 1) * d)
        logits = _dot(xb, wg_ref[:, cols]) + bg_ref[:, br * d:(br + 1) * d]
        term = jax.nn.sigmoid(logits) * proj
        merged = term if merged is None else merged + term
    mix = _dot(_bf16(merged), wo_ref[...])
    y_ref[...] = _layer_norm(alpha * x + mix, g_ref[...], b_ref[...])


def _mix(x2, oa, obt, oc, w_in, bg, wa, wb, wc, wo, g, b, layer, alpha, seq):
    n, d = x2.shape
    tm = ROW_TILE
    nt = seq // tm
    row = lambda i: (i, 0)
    const = lambda i: (0, 0)
    full = lambda a: pl.BlockSpec(a.shape, const)
    return pl.pallas_call(
        functools.partial(_mix_kernel, alpha=alpha), grid=(n // tm,),
        in_specs=[
            pl.BlockSpec((tm, d), row), pl.BlockSpec((tm, A_WIDTH), row),
            pl.BlockSpec((1, B_Q_WIDTH, tm), lambda i: (i // nt, 0, i % nt)),
            pl.BlockSpec((tm, C_WIDTH), row),
            _layer_spec(w_in, layer), full(bg), _layer_spec(wa, layer), _layer_spec(wb, layer),
            _layer_spec(wc, layer), _layer_spec(wo, layer), full(g), full(b),
        ],
        out_specs=pl.BlockSpec((tm, d), row),
        out_shape=jax.ShapeDtypeStruct((n, d), jnp.float32),
        compiler_params=_params("parallel"), name="mix",
    )(x2, oa, obt, oc, w_in, bg, wa, wb, wc, wo, g, b)


def _mlp_kernel(x_ref, wu_ref, wd_ref, g_ref, b_ref, y_ref, *, alpha, chunk):
    x = x_ref[...]
    xb = _bf16(x)
    ff = None
    for c in range(wu_ref.shape[1] // chunk):
        hid = jnp.square(jnp.maximum(_dot(xb, wu_ref[:, c * chunk:(c + 1) * chunk]), 0.0))
        part = _dot(_bf16(hid), wd_ref[c * chunk:(c + 1) * chunk, :])
        ff = part if ff is None else ff + part
    y_ref[...] = _layer_norm(alpha * x + ff, g_ref[...], b_ref[...])


def _mlp(x2, wu, wd, g, b, layer, alpha):
    n, d = x2.shape
    tm = ROW_TILE
    row = lambda i: (i, 0)
    const = lambda i: (0, 0)
    return pl.pallas_call(
        functools.partial(_mlp_kernel, alpha=alpha, chunk=1024), grid=(n // tm,),
        in_specs=[
            pl.BlockSpec((tm, d), row),
            _layer_spec(wu, layer), _layer_spec(wd, layer),
            pl.BlockSpec(g.shape, const), pl.BlockSpec(b.shape, const),
        ],
        out_specs=pl.BlockSpec((tm, d), row),
        out_shape=jax.ShapeDtypeStruct((n, d), jnp.float32),
        compiler_params=_params("parallel"), name="mlp",
    )(x2, wu, wd, g, b)


def _rotary_tables(pos_list, theta_list, half, width_per_head, reps):
    s = pos_list[0].shape[0]
    c = jnp.ones((s, width_per_head), jnp.float32)
    sa = jnp.zeros((s, width_per_head), jnp.float32)
    sb = jnp.zeros((s, width_per_head), jnp.float32)
    for i, (pos, theta) in enumerate(zip(pos_list, theta_list)):
        inv = theta ** (-jnp.arange(half, dtype=jnp.float32) / half)
        ang = pos.astype(jnp.float32)[:, None] * inv[None, :]
        cos, sin = jnp.cos(ang), jnp.sin(ang)
        o = 2 * half * i
        c = c.at[:, o:o + half].set(cos).at[:, o + half:o + 2 * half].set(cos)
        sa = sa.at[:, o:o + half].set(-sin)
        sb = sb.at[:, o + half:o + 2 * half].set(sin)
    tile = lambda t: jnp.tile(t, (1, reps))
    return tile(c), tile(sa), tile(sb)


def kernel(x, w_in, b_gate, q_norm_b, k_norm_b, rpb_c, w_branch_a, w_branch_b, w_branch_c, w_out,
           ln1_g, ln1_b, w_up, w_down, ln2_g, ln2_b):
    batch, seq, d = x.shape
    depth = w_in.shape[0]
    alpha = (2 * depth) ** 0.25
    rows = seq // GRID_W
    assert seq % (ROW_TILE) == 0 and rows % C_TILE_ROWS == 0 and rows >= 2 * _c_window_rows()
    assert all(window // (2 * dil) == A_RADIUS and (seq // dil) >= Q_BLOCK + 2 * A_RADIUS
               for window, dil in A_CONFIGS)
    assert A_CONFIGS[0][1] == 1 and seq % A_STEP == 0 and len(A_CONFIGS) == 3

    pos = jnp.arange(seq)
    tabs_a = _rotary_tables([pos], [A_ROPE_THETA], A_ROPE_DIMS // 2, HEAD_DIM, A_HEADS)
    tabs_b = _rotary_tables([pos // GRID_W, pos % GRID_W], [B_AXIAL_THETA, B_AXIAL_THETA],
                            HEAD_DIM // 4, HEAD_DIM, B_KV_HEADS)
    head_of = np.arange(B_Q_WIDTH) // HEAD_DIM
    ones_blk = jnp.asarray(head_of[:, None] == head_of[None, :], jnp.bfloat16)

    bf = jnp.bfloat16
    w_in, w_branch_a, w_branch_b, w_branch_c, w_out, w_up, w_down = (
        w.astype(bf) for w in (w_in, w_branch_a, w_branch_b, w_branch_c, w_out, w_up, w_down))
    c_bias = _c_bias_tables(rpb_c, rows)
    x2 = x.reshape(batch * seq, d)
    for layer in range(depth):
        qg = jnp.tile(q_norm_b[layer], B_Q_HEADS)[None, :]
        kg = jnp.tile(k_norm_b[layer], B_KV_HEADS)[None, :]
        a1, a4, a16, qbt, kb, vbt, qc, kc, vc = _in_proj(
            x2, w_in, layer, tabs_a, tabs_b, qg, kg, ones_blk, batch, seq)
        oa = _attn_a((a1, a4, a16), batch, seq)
        obt = _attn_b(qbt, kb, vbt, batch, seq)
        oc = _attn_c(qc, kc, vc, c_bias, layer, batch, seq)
        x2 = _mix(x2, oa, obt, oc, w_in, b_gate[layer][None, :],
                  w_branch_a, w_branch_b, w_branch_c, w_out,
                  ln1_g[layer][None, :], ln1_b[layer][None, :], layer, alpha, seq)
        x2 = _mlp(x2, w_up, w_down, ln2_g[layer][None, :], ln2_b[layer][None, :], layer, alpha)
    return x2.reshape(batch, seq, d)
```

```python
import functools
import math

import numpy as np
import jax
import jax.numpy as jnp
from jax import lax
from jax.experimental import pallas as pl
from jax.experimental.pallas import tpu as pltpu

HEAD_DIM = 64
A_HEADS = 4
A_CONFIGS = ((128, 1), (512, 4), (2048, 16))
A_ROPE_DIMS = HEAD_DIM // 4
A_ROPE_THETA = 500000.0
B_Q_HEADS = 8
B_KV_HEADS = 2
B_AXIAL_THETA = 10000.0
C_HEADS = 4
C_ROWS_MAX = 8
C_COLS = 16
GRID_W = 64
Q_BLOCK = 128
LN_EPS = 1e-5
RMS_EPS = 1e-6
NEG_INF = -1e30
LOG2E = math.log2(math.e)

A_WIDTH = A_HEADS * HEAD_DIM
B_Q_WIDTH = B_Q_HEADS * HEAD_DIM
B_KV_WIDTH = B_KV_HEADS * HEAD_DIM
C_WIDTH = C_HEADS * HEAD_DIM
QKV_COLS = 3 * A_WIDTH + B_Q_WIDTH + 2 * B_KV_WIDTH + 3 * C_WIDTH

VMEM_LIMIT_BYTES = 56 * 1024 * 1024
ROW_TILE = 1024
B_TILE = 256
B_VT_ROWS = HEAD_DIM + 16
B_QTILE = 512
B_UNROLL = 8
C_TILE_ROWS = 4
A_RADIUS = A_CONFIGS[0][0] // (2 * A_CONFIGS[0][1])
A_STEP = Q_BLOCK * max(dil for _, dil in A_CONFIGS)
A_INTERLEAVE = 4
A_GROUPS_PER_TRIP = 4

_NT = (((1,), (1,)), ((), ()))
_TN = (((0,), (0,)), ((), ()))


def _params(*sem):
    return pltpu.CompilerParams(dimension_semantics=sem, vmem_limit_bytes=VMEM_LIMIT_BYTES)


def _bf16(x):
    return x.astype(jnp.bfloat16)


def _dot(a, b):
    return jnp.dot(a, b, preferred_element_type=jnp.float32)


def _layer_norm(y, g, b):
    mu = jnp.mean(y, axis=-1, keepdims=True)
    d = y - mu
    var = jnp.mean(d * d, axis=-1, keepdims=True)
    return d * lax.rsqrt(var + LN_EPS) * g + b


def _rotate(x, c, sa, sb, half):
    w = x.shape[-1]
    return x * c + pltpu.roll(x, w - half, 1) * sa + pltpu.roll(x, half, 1) * sb


def _head_mean_sq(h, ones_blk):
    sq = h * h
    hi = _bf16(sq)
    lo = _bf16(sq - hi.astype(jnp.float32))
    return (_dot(hi, ones_blk) + _dot(lo, ones_blk)) * (1.0 / HEAD_DIM)


def _in_proj_kernel(x_ref, w_ref, ca_ref, saa_ref, sba_ref, cb_ref, sab_ref, sbb_ref,
                    qg_ref, kg_ref, ones_ref,
                    a1_ref, a4_ref, a16_ref, qbt_ref, kb_ref, vbt_ref, qc_ref, kc_ref, vc_ref,
                    stage_sc):
    xb = _bf16(x_ref[...])
    scale = HEAD_DIM ** -0.5

    off_b = 3 * A_WIDTH
    off_c = off_b + B_Q_WIDTH + 2 * B_KV_WIDTH

    ha = _dot(xb, w_ref[:, 0:off_b])
    ca, saa, sba = ca_ref[...], saa_ref[...], sba_ref[...]
    half_a = A_ROPE_DIMS // 2
    qa = _rotate(ha[:, 0:A_WIDTH], ca, saa, sba, half_a) * (scale * LOG2E)
    ka = _rotate(ha[:, A_WIDTH:2 * A_WIDTH], ca, saa, sba, half_a)
    qkv_a = (qa, ka, ha[:, 2 * A_WIDTH:3 * A_WIDTH])
    for part, val in enumerate(qkv_a):
        a1_ref[0, :, part * A_WIDTH:(part + 1) * A_WIDTH] = _bf16(val)
        for s in range(A_WIDTH // 128):
            stage_sc[part * (A_WIDTH // 128) + s] = val[:, s * 128:(s + 1) * 128]
    tm = x_ref.shape[0]
    for out_ref in (a4_ref, a16_ref):
        dil = out_ref.shape[1]
        for r in range(dil):
            for s in range(stage_sc.shape[0]):
                out_ref[0, r, :, s * 128:(s + 1) * 128] = _bf16(
                    stage_sc[s, pl.ds(r, tm // dil, stride=dil), :])

    hb = _dot(xb, w_ref[:, off_b:off_c])
    ones_q = ones_ref[...]
    ones_k = ones_ref[0:B_KV_WIDTH, 0:B_KV_WIDTH]
    cb, sab, sbb = cb_ref[...], sab_ref[...], sbb_ref[...]
    rep = B_Q_WIDTH // B_KV_WIDTH
    cb_q = jnp.concatenate([cb] * rep, axis=1)
    sab_q = jnp.concatenate([sab] * rep, axis=1)
    sbb_q = jnp.concatenate([sbb] * rep, axis=1)
    half_b = HEAD_DIM // 4
    qb = hb[:, 0:B_Q_WIDTH]
    qb = qb * lax.rsqrt(_head_mean_sq(qb, ones_q) + RMS_EPS) * qg_ref[...]
    qb = _rotate(qb, cb_q, sab_q, sbb_q, half_b) * (scale * LOG2E)
    kb = hb[:, B_Q_WIDTH:B_Q_WIDTH + B_KV_WIDTH]
    kb = kb * lax.rsqrt(_head_mean_sq(kb, ones_k) + RMS_EPS) * kg_ref[...]
    kb = _rotate(kb, cb, sab, sbb, half_b)
    vb = hb[:, B_Q_WIDTH + B_KV_WIDTH:]
    qbt_ref[0] = _bf16(qb.T)
    kb_ref[...] = _bf16(kb)
    ones_rows = jnp.ones((B_VT_ROWS - HEAD_DIM, B_TILE), jnp.bfloat16)
    for c in range(vbt_ref.shape[1]):
        vt = _bf16(vb[c * B_TILE:(c + 1) * B_TILE, :].T)
        for g in range(B_KV_HEADS):
            vbt_ref[0, c, g * B_VT_ROWS:g * B_VT_ROWS + HEAD_DIM, :] = vt[g * HEAD_DIM:(g + 1) * HEAD_DIM]
            vbt_ref[0, c, g * B_VT_ROWS + HEAD_DIM:(g + 1) * B_VT_ROWS, :] = ones_rows

    hc = _dot(xb, w_ref[:, off_c:off_c + 3 * C_WIDTH])
    qc_ref[...] = _bf16(hc[:, 0:C_WIDTH] * (scale * LOG2E))
    kc_ref[...] = _bf16(hc[:, C_WIDTH:2 * C_WIDTH])
    vc_ref[...] = _bf16(hc[:, 2 * C_WIDTH:3 * C_WIDTH])


def _layer_spec(stacked, layer):
    zeros = (0,) * (stacked.ndim - 1)
    return pl.BlockSpec((None,) + stacked.shape[1:], lambda *_: (layer,) + zeros,
                        pipeline_mode=pl.Buffered(1))


def _in_proj(x2, w_in, layer, tabs_a, tabs_b, qg, kg, ones_blk, batch, seq):
    n, d = x2.shape
    tm = ROW_TILE
    nt = seq // tm
    row = lambda i: (i, 0)
    pos = lambda i: (i % nt, 0)
    const = lambda i: (0, 0)
    bf = jnp.bfloat16
    out_shape = (
        jax.ShapeDtypeStruct((batch, seq, 3 * A_WIDTH), bf),
        *(jax.ShapeDtypeStruct((batch, dil, seq // dil, 3 * A_WIDTH), bf) for _, dil in A_CONFIGS[1:]),
        jax.ShapeDtypeStruct((batch, B_Q_WIDTH, seq), bf),
        jax.ShapeDtypeStruct((n, B_KV_WIDTH), bf),
        jax.ShapeDtypeStruct((batch, seq // B_TILE, B_KV_HEADS * B_VT_ROWS, B_TILE), bf),
        jax.ShapeDtypeStruct((n, C_WIDTH), bf), jax.ShapeDtypeStruct((n, C_WIDTH), bf),
        jax.ShapeDtypeStruct((n, C_WIDTH), bf),
    )
    out_specs = (
        pl.BlockSpec((1, tm, 3 * A_WIDTH), lambda i: (i // nt, i % nt, 0)),
        *(pl.BlockSpec((1, dil, tm // dil, 3 * A_WIDTH), lambda i: (i // nt, 0, i % nt, 0))
          for _, dil in A_CONFIGS[1:]),
        pl.BlockSpec((1, B_Q_WIDTH, tm), lambda i: (i // nt, 0, i % nt)),
        pl.BlockSpec((tm, B_KV_WIDTH), row),
        pl.BlockSpec((1, tm // B_TILE, B_KV_HEADS * B_VT_ROWS, B_TILE),
                     lambda i: (i // nt, i % nt, 0, 0)),
        pl.BlockSpec((tm, C_WIDTH), row), pl.BlockSpec((tm, C_WIDTH), row),
        pl.BlockSpec((tm, C_WIDTH), row),
    )
    in_specs = [
        pl.BlockSpec((tm, d), row),
        _layer_spec(w_in, layer),
        pl.BlockSpec((tm, A_WIDTH), pos), pl.BlockSpec((tm, A_WIDTH), pos),
        pl.BlockSpec((tm, A_WIDTH), pos),
        pl.BlockSpec((tm, B_KV_WIDTH), pos), pl.BlockSpec((tm, B_KV_WIDTH), pos),
        pl.BlockSpec((tm, B_KV_WIDTH), pos),
        pl.BlockSpec(qg.shape, const), pl.BlockSpec(kg.shape, const),
        pl.BlockSpec(ones_blk.shape, const),
    ]
    return pl.pallas_call(
        _in_proj_kernel, grid=(n // tm,), in_specs=in_specs, out_specs=out_specs,
        out_shape=out_shape,
        scratch_shapes=[pltpu.VMEM((3 * A_WIDTH // 128, tm, 128), jnp.float32)],
        compiler_params=_params("parallel"), name="in_proj",
    )(x2, w_in, *tabs_a, *tabs_b, qg, kg, ones_blk)


def _attn_a_kernel(*refs, seq):
    n_cfg = len(A_CONFIGS)
    qkv_refs = [refs[3 * c:3 * c + 3] for c in range(n_cfg)]
    bias_ref, hmask_ref, o_ref, o_sc, m_sc, l_sc = refs[3 * n_cfg:]
    i = pl.program_id(1)
    span = Q_BLOCK + 2 * A_RADIUS
    lane_head = lax.broadcasted_iota(jnp.int32, (Q_BLOCK, A_WIDTH), 1) // HEAD_DIM

    def widen(t, width):
        return t if t.shape[1] in (1, width) else jnp.concatenate([t] * (width // t.shape[1]), axis=1)

    def per_head_lanes(stacked):
        tiles = [widen(stacked[h * Q_BLOCK:(h + 1) * Q_BLOCK], A_WIDTH) for h in range(A_HEADS)]
        out = tiles[-1]
        for h in range(A_HEADS - 2, -1, -1):
            out = jnp.where(lane_head == h, tiles[h], out)
        return out

    def scores(blk):
        halves = []
        for h0 in range(0, A_HEADS, A_HEADS // 2):
            qm = jnp.concatenate([blk["q"] * hmask_ref[h] for h in range(h0, h0 + A_HEADS // 2)], axis=0)
            halves.append(lax.dot_general(qm, blk["kw"], _NT, preferred_element_type=jnp.float32))
        return jnp.concatenate(halves, axis=0)

    def softmax(blk, s, first):
        rows = blk["rows"]
        s = s + jnp.concatenate([bias_ref[blk["edge"]]] * A_HEADS, axis=0)
        m_new = jnp.max(s, axis=-1, keepdims=True)
        alpha = None
        if not first:
            m_prev = jnp.concatenate([m_sc[h, rows, :] for h in range(A_HEADS)], axis=0)
            l_prev = jnp.concatenate([l_sc[h, rows, :] for h in range(A_HEADS)], axis=0)
            m_new = jnp.maximum(m_prev, m_new)
            alpha = jnp.exp2(m_prev - m_new)
        p = jnp.exp2(s - widen(m_new, span))
        l_new = jnp.sum(p, axis=-1, keepdims=True)
        if not first:
            l_new = l_new + alpha * l_prev
        return _bf16(p), m_new, l_new, alpha

    def update(blk, pv, m_new, l_new, alpha, first, last):
        rows = blk["rows"]
        o_new = per_head_lanes(pv)
        if not first:
            o_prev = jnp.concatenate([o_sc[s, rows, :] for s in range(o_sc.shape[0])], axis=1)
            o_new = o_new + per_head_lanes(alpha) * o_prev
        if last:
            o_new = o_new / per_head_lanes(l_new)
        else:
            for h in range(A_HEADS):
                hs = slice(h * Q_BLOCK, (h + 1) * Q_BLOCK)
                m_sc[h, rows, :] = jnp.broadcast_to(m_new[hs], (Q_BLOCK, 128))
                l_sc[h, rows, :] = jnp.broadcast_to(l_new[hs], (Q_BLOCK, 128))
        for s in range(o_sc.shape[0]):
            o_sc[s, rows, :] = o_new[:, s * 128:(s + 1) * 128]

    def process(blocks, first, last):
        ss = [scores(blk) for blk in blocks]
        res = []
        for blk, s in zip(blocks, ss):
            p, m_new, l_new, alpha = softmax(blk, s, first)
            half = p.shape[0] // 2
            pv = jnp.concatenate([_dot(p[:half], blk["vw"]), _dot(p[half:], blk["vw"])], axis=0)
            res.append((pv, m_new, l_new, alpha))
        for blk, (pv, m_new, l_new, alpha) in zip(blocks, res):
            update(blk, pv, m_new, l_new, alpha, first, last)

    for ci, (_, dil) in enumerate(A_CONFIGS):
        q_ref, k_ref, v_ref = qkv_refs[ci]
        sub_len = seq // dil
        blocks_per_res = A_STEP // (dil * Q_BLOCK)
        first, last = ci == 0, ci == n_cfg - 1

        def block(idx, dil=dil, q_ref=q_ref, k_ref=k_ref, v_ref=v_ref, sub_len=sub_len,
                  blocks_per_res=blocks_per_res):
            r, qb = idx // blocks_per_res, idx % blocks_per_res
            base = i * (A_STEP // dil) + qb * Q_BLOCK
            start = pl.multiple_of(jnp.clip(base - A_RADIUS, 0, sub_len - span), A_RADIUS)
            q_rows = pl.ds(pl.multiple_of(qb * Q_BLOCK, Q_BLOCK), Q_BLOCK)
            if dil == 1:
                q, kw, vw = q_ref[0, q_rows, :], k_ref[0, pl.ds(start, span), :], v_ref[0, pl.ds(start, span), :]
                rows = q_rows
            else:
                q, kw, vw = (q_ref[0, r, q_rows, :], k_ref[0, r, pl.ds(start, span), :],
                             v_ref[0, r, pl.ds(start, span), :])
                rows = pl.ds(qb * Q_BLOCK * dil + r, Q_BLOCK, stride=dil)
            return dict(q=q, kw=kw, vw=vw, rows=rows, edge=(base - start) // A_RADIUS)

        def trip(t, carry, block=block, first=first, last=last):
            for u in range(A_GROUPS_PER_TRIP):
                first_blk = A_INTERLEAVE * (A_GROUPS_PER_TRIP * t + u)
                process([block(first_blk + v) for v in range(A_INTERLEAVE)], first, last)
            return carry

        lax.fori_loop(0, A_STEP // Q_BLOCK // (A_INTERLEAVE * A_GROUPS_PER_TRIP), trip, 0)

    o_ref[0] = _bf16(jnp.concatenate([o_sc[s] for s in range(o_sc.shape[0])], axis=1))


def _attn_a(a_qkv, batch, seq):
    single = pl.Buffered(1)
    args, in_specs = [], []
    for (_, dil), arr in zip(A_CONFIGS, a_qkv):
        sub_len = seq // dil
        if dil == 1:
            q_spec = pl.BlockSpec((1, A_STEP, A_WIDTH), lambda b, i: (b, i, 0))
            kv_spec = lambda part: pl.BlockSpec((1, seq, A_WIDTH), lambda b, i: (b, 0, part),
                                                pipeline_mode=single)
        else:
            q_spec = pl.BlockSpec((1, dil, A_STEP // dil, A_WIDTH), lambda b, i: (b, 0, i, 0))
            kv_spec = lambda part, dil=dil, sub_len=sub_len: pl.BlockSpec(
                (1, dil, sub_len, A_WIDTH), lambda b, i: (b, 0, 0, part), pipeline_mode=single)
        args += [arr, arr, arr]
        in_specs += [q_spec, kv_spec(1), kv_spec(2)]
    span = Q_BLOCK + 2 * A_RADIUS
    rel = np.arange(Q_BLOCK)[:, None] - np.arange(span)[None, :]
    bias = jnp.asarray(np.stack([np.where(np.abs(rel + e * A_RADIUS) <= A_RADIUS, 0.0, NEG_INF)
                                 for e in range(3)]), jnp.float32)
    hmask = jnp.asarray((np.arange(A_WIDTH)[None, :] // HEAD_DIM == np.arange(A_HEADS)[:, None])[:, None, :],
                        jnp.bfloat16)
    const3 = lambda b, i: (0, 0, 0)
    out = pl.pallas_call(
        functools.partial(_attn_a_kernel, seq=seq), grid=(batch, seq // A_STEP),
        in_specs=in_specs + [pl.BlockSpec(bias.shape, const3), pl.BlockSpec(hmask.shape, const3)],
        out_specs=pl.BlockSpec((1, A_STEP, A_WIDTH), lambda b, i: (b, i, 0)),
        out_shape=jax.ShapeDtypeStruct((batch, seq, A_WIDTH), jnp.bfloat16),
        scratch_shapes=[pltpu.VMEM((A_WIDTH // 128, A_STEP, 128), jnp.float32),
                        pltpu.VMEM((A_HEADS, A_STEP, 128), jnp.float32),
                        pltpu.VMEM((A_HEADS, A_STEP, 128), jnp.float32)],
        compiler_params=_params("parallel", "arbitrary"), name="attn_a",
    )(*args, bias, hmask)
    return out.reshape(batch * seq, A_WIDTH)


def _attn_b_kernel(qt_ref, k_ref, vt_ref, ot_ref, wq_sc, m_sc, acc_sc, st_sc, mx_sc):
    assert B_UNROLL % 2 == 0 and vt_ref.shape[1] % B_UNROLL == 0
    n_kt = vt_ref.shape[1]
    group = B_Q_HEADS // B_KV_HEADS
    wq_sc[...] = jnp.zeros(wq_sc.shape, wq_sc.dtype)
    for h in range(B_Q_HEADS):
        g = h // group
        wq_sc[h, g * HEAD_DIM:(g + 1) * HEAD_DIM, :] = qt_ref[0, h * HEAD_DIM:(h + 1) * HEAD_DIM, :]
    m_sc[...] = jnp.full(m_sc.shape, NEG_INF, jnp.float32)
    acc_sc[...] = jnp.zeros(acc_sc.shape, jnp.float32)

    def k_tile(j):
        return k_ref[0, pl.ds(pl.multiple_of(j * B_TILE, B_TILE), B_TILE), :]

    def softmax_pv(j, slot, h):
        g = h // group
        vt = vt_ref[0, j, g * B_VT_ROWS:(g + 1) * B_VT_ROWS, :]
        m_prev = m_sc[h]
        m_new = jnp.maximum(m_prev, mx_sc[slot, h])
        alpha = jnp.exp2(m_prev - m_new)
        p = jnp.exp2(st_sc[slot, h] - m_new)
        acc_sc[h] = alpha * acc_sc[h] + _dot(vt, _bf16(p))
        m_sc[h] = m_new

    def score_tile(k, slot, h):
        st = _dot(k, wq_sc[h])
        st_sc[slot, h] = st
        mx_sc[slot, h] = jnp.max(st, axis=0, keepdims=True)

    for h in range(B_Q_HEADS):
        score_tile(k_tile(0), 0, h)

    def step(j, slot):
        k_next = k_tile(jnp.minimum(j + 1, n_kt - 1))
        for h in range(B_Q_HEADS):
            score_tile(k_next, 1 - slot, h)
            softmax_pv(j, slot, h)

    def body(jj, carry):
        for u in range(B_UNROLL):
            step(B_UNROLL * jj + u, u % 2)
        return carry

    lax.fori_loop(0, n_kt // B_UNROLL, body, 0)
    for h in range(B_Q_HEADS):
        ot_ref[0, h * HEAD_DIM:(h + 1) * HEAD_DIM, :] = _bf16(
            acc_sc[h, 0:HEAD_DIM] / acc_sc[h, HEAD_DIM:HEAD_DIM + 1])


def _attn_b(qbt, kb, vbt, batch, seq):
    tq = B_QTILE
    group = B_Q_HEADS // B_KV_HEADS
    kb3 = kb.reshape(batch, seq, B_KV_WIDTH)
    return pl.pallas_call(
        _attn_b_kernel, grid=(batch, seq // tq),
        in_specs=[
            pl.BlockSpec((1, B_Q_WIDTH, tq), lambda b, i: (b, 0, i)),
            pl.BlockSpec((1, seq, B_KV_WIDTH), lambda b, i: (b, 0, 0)),
            pl.BlockSpec((1, seq // B_TILE, B_KV_HEADS * B_VT_ROWS, B_TILE), lambda b, i: (b, 0, 0, 0)),
        ],
        out_specs=pl.BlockSpec((1, B_Q_WIDTH, tq), lambda b, i: (b, 0, i)),
        out_shape=jax.ShapeDtypeStruct((batch, B_Q_WIDTH, seq), jnp.bfloat16),
        scratch_shapes=[
            pltpu.VMEM((B_Q_HEADS, B_KV_WIDTH, tq), jnp.bfloat16),
            pltpu.VMEM((B_Q_HEADS, 1, tq), jnp.float32),
            pltpu.VMEM((B_Q_HEADS, B_VT_ROWS, tq), jnp.float32),
            pltpu.VMEM((2, B_Q_HEADS, B_TILE, tq), jnp.float32),
            pltpu.VMEM((2, B_Q_HEADS, 1, tq), jnp.float32),
        ],
        compiler_params=_params("parallel", "parallel"), name="attn_b",
    )(qbt, kb3, vbt)


def _c_window_rows():
    return C_TILE_ROWS + C_ROWS_MAX - 1


def _attn_c_kernel(q_ref, k_ref, v_ref, bias_ref, o_ref, *, rows):
    t = pl.program_id(1)
    win = _c_window_rows() * GRID_W
    ws = jnp.clip(t * C_TILE_ROWS - C_ROWS_MAX // 2, 0, rows - _c_window_rows())
    start = pl.multiple_of(ws * GRID_W, GRID_W)
    kw = k_ref[0, pl.ds(start, win), :]
    vw = v_ref[0, pl.ds(start, win), :]
    q = q_ref[0]
    lane_head = lax.broadcasted_iota(jnp.int32, q.shape, 1) // HEAD_DIM
    out = jnp.zeros(q.shape, jnp.float32)
    for h in range(C_HEADS):
        qh = jnp.where(lane_head == h, q, jnp.zeros_like(q))
        s = lax.dot_general(qh, kw, _NT, preferred_element_type=jnp.float32) + bias_ref[0, h]
        m = jnp.max(s, axis=-1, keepdims=True)
        p = jnp.exp2(s - m)
        l = jnp.sum(p, axis=-1, keepdims=True)
        pv = _dot(_bf16(p), vw)
        out = jnp.where(lane_head == h, pv / l, out)
    o_ref[0] = _bf16(out)


def _c_bias_tables(rpb, rows):
    n_tiles = rows // C_TILE_ROWS
    wr = _c_window_rows()
    kr = min(C_ROWS_MAX, rows)
    c = np.arange(GRID_W)
    c0 = np.clip(c - C_COLS // 2, 0, GRID_W - C_COLS)
    col_ok = (c[None, :] >= c0[:, None]) & (c[None, :] < c0[:, None] + C_COLS)
    pad = GRID_W - C_COLS
    padded = jnp.pad(rpb.astype(jnp.float32) * LOG2E, ((0, 0), (0, 0), (0, 0), (pad, pad)))
    shifted = jnp.stack([padded[..., GRID_W - 1 - qc:2 * GRID_W - 1 - qc] for qc in range(GRID_W)],
                        axis=-2)
    by_dr = jnp.where(col_ok, shifted, NEG_INF)
    masked = jnp.full(by_dr.shape[:2] + (GRID_W, GRID_W), NEG_INF, jnp.float32)
    tabs = []
    for t in (0, 1, n_tiles - 1):
        ws = int(np.clip(t * C_TILE_ROWS - C_ROWS_MAX // 2, 0, rows - wr))
        q_blocks = []
        for r in range(t * C_TILE_ROWS, (t + 1) * C_TILE_ROWS):
            r0 = int(np.clip(r - kr // 2, 0, rows - kr))
            k_blocks = [by_dr[:, :, krow - r + C_ROWS_MAX - 1] if r0 <= krow < r0 + kr else masked
                        for krow in range(ws, ws + wr)]
            q_blocks.append(jnp.concatenate(k_blocks, axis=-1))
        tabs.append(jnp.concatenate(q_blocks, axis=-2))
    return jnp.stack(tabs, axis=1)


def _attn_c(qc, kc, vc, bias_tabs, layer, batch, seq):
    rows = seq // GRID_W
    n_tiles = rows // C_TILE_ROWS
    tq = C_TILE_ROWS * GRID_W
    view = lambda t: t.reshape(batch, seq, C_WIDTH)
    cls = lambda b, t: (layer, jnp.where(t == 0, 0, jnp.where(t == n_tiles - 1, 2, 1)), 0, 0, 0)
    out = pl.pallas_call(
        functools.partial(_attn_c_kernel, rows=rows), grid=(batch, n_tiles),
        in_specs=[
            pl.BlockSpec((1, tq, C_WIDTH), lambda b, t: (b, t, 0)),
            pl.BlockSpec((1, seq, C_WIDTH), lambda b, t: (b, 0, 0)),
            pl.BlockSpec((1, seq, C_WIDTH), lambda b, t: (b, 0, 0)),
            pl.BlockSpec((None, 1) + bias_tabs.shape[2:], cls),
        ],
        out_specs=pl.BlockSpec((1, tq, C_WIDTH), lambda b, t: (b, t, 0)),
        out_shape=jax.ShapeDtypeStruct((batch, seq, C_WIDTH), jnp.bfloat16),
        compiler_params=_params("parallel", "arbitrary"), name="attn_c",
    )(view(qc), view(kc), view(vc), bias_tabs)
    return out.reshape(batch * seq, C_WIDTH)


def _mix_kernel(x_ref, oa_ref, obt_ref, oc_ref, wg_ref, bg_ref, wa_ref, wb_ref, wc_ref, wo_ref,
                g_ref, b_ref, y_ref, *, alpha):
    x = x_ref[...]
    xb = _bf16(x)
    d = x.shape[1]
    projs = (
        _dot(oa_ref[...], wa_ref[...]),
        lax.dot_general(obt_ref[0], wb_ref[...], _TN, preferred_element_type=jnp.float32),
        _dot(oc_ref[...], wc_ref[...]),
    )
    merged = None
    for br, proj in enumerate(projs):
        cols = slice(QKV_COLS + br * d, QKV_COLS + (br + 1) * d)
        logits = _dot(xb, wg_ref[:, cols]) + bg_ref[:, br * d:(br + 1) * d]
        term = jax.nn.sigmoid(logits) * proj
        merged = term if merged is None else merged + term
    mix = _dot(_bf16(merged), wo_ref[...])
    y_ref[...] = _layer_norm(alpha * x + mix, g_ref[...], b_ref[...])


def _mix(x2, oa, obt, oc, w_in, bg, wa, wb, wc, wo, g, b, layer, alpha, seq):
    n, d = x2.shape
    tm = ROW_TILE
    nt = seq // tm
    row = lambda i: (i, 0)
    const = lambda i: (0, 0)
    full = lambda a: pl.BlockSpec(a.shape, const)
    return pl.pallas_call(
        functools.partial(_mix_kernel, alpha=alpha), grid=(n // tm,),
        in_specs=[
            pl.BlockSpec((tm, d), row), pl.BlockSpec((tm, A_WIDTH), row),
            pl.BlockSpec((1, B_Q_WIDTH, tm), lambda i: (i // nt, 0, i % nt)),
            pl.BlockSpec((tm, C_WIDTH), row),
            _layer_spec(w_in, layer), full(bg), _layer_spec(wa, layer), _layer_spec(wb, layer),
            _layer_spec(wc, layer), _layer_spec(wo, layer), full(g), full(b),
        ],
        out_specs=pl.BlockSpec((tm, d), row),
        out_shape=jax.ShapeDtypeStruct((n, d), jnp.float32),
        compiler_params=_params("parallel"), name="mix",
    )(x2, oa, obt, oc, w_in, bg, wa, wb, wc, wo, g, b)


def _mlp_kernel(x_ref, wu_ref, wd_ref, g_ref, b_ref, y_ref, *, alpha, chunk):
    x = x_ref[...]
    xb = _bf16(x)
    ff = None
    for c in range(wu_ref.shape[1] // chunk):
        hid = jnp.square(jnp.maximum(_dot(xb, wu_ref[:, c * chunk:(c + 1) * chunk]), 0.0))
        part = _dot(_bf16(hid), wd_ref[c * chunk:(c + 1) * chunk, :])
        ff = part if ff is None else ff + part
    y_ref[...] = _layer_norm(alpha * x + ff, g_ref[...], b_ref[...])


def _mlp(x2, wu, wd, g, b, layer, alpha):
    n, d = x2.shape
    tm = ROW_TILE
    row = lambda i: (i, 0)
    const = lambda i: (0, 0)
    return pl.pallas_call(
        functools.partial(_mlp_kernel, alpha=alpha, chunk=1024), grid=(n // tm,),
        in_specs=[
            pl.BlockSpec((tm, d), row),
            _layer_spec(wu, layer), _layer_spec(wd, layer),
            pl.BlockSpec(g.shape, const), pl.BlockSpec(b.shape, const),
        ],
        out_specs=pl.BlockSpec((tm, d), row),
        out_shape=jax.ShapeDtypeStruct((n, d), jnp.float32),
        compiler_params=_params("parallel"), name="mlp",
    )(x2, wu, wd, g, b)


def _rotary_tables(pos_list, theta_list, half, width_per_head, reps):
    s = pos_list[0].shape[0]
    c = jnp.ones((s, width_per_head), jnp.float32)
    sa = jnp.zeros((s, width_per_head), jnp.float32)
    sb = jnp.zeros((s, width_per_head), jnp.float32)
    for i, (pos, theta) in enumerate(zip(pos_list, theta_list)):
        inv = theta ** (-jnp.arange(half, dtype=jnp.float32) / half)
        ang = pos.astype(jnp.float32)[:, None] * inv[None, :]
        cos, sin = jnp.cos(ang), jnp.sin(ang)
        o = 2 * half * i
        c = c.at[:, o:o + half].set(cos).at[:, o + half:o + 2 * half].set(cos)
        sa = sa.at[:, o:o + half].set(-sin)
        sb = sb.at[:, o + half:o + 2 * half].set(sin)
    tile = lambda t: jnp.tile(t, (1, reps))
    return tile(c), tile(sa), tile(sb)


def kernel(x, w_in, b_gate, q_norm_b, k_norm_b, rpb_c, w_branch_a, w_branch_b, w_branch_c, w_out,
           ln1_g, ln1_b, w_up, w_down, ln2_g, ln2_b):
    batch, seq, d = x.shape
    depth = w_in.shape[0]
    alpha = (2 * depth) ** 0.25
    rows = seq // GRID_W
    assert seq % ROW_TILE == 0 and seq % B_QTILE == 0 and (seq // B_TILE) % B_UNROLL == 0
    assert rows % C_TILE_ROWS == 0 and rows >= 2 * _c_window_rows()
    assert all(window // (2 * dil) == A_RADIUS and (seq // dil) >= Q_BLOCK + 2 * A_RADIUS
               for window, dil in A_CONFIGS)
    assert A_CONFIGS[0][1] == 1 and seq % A_STEP == 0 and len(A_CONFIGS) == 3

    pos = jnp.arange(seq)
    tabs_a = _rotary_tables([pos], [A_ROPE_THETA], A_ROPE_DIMS // 2, HEAD_DIM, A_HEADS)
    tabs_b = _rotary_tables([pos // GRID_W, pos % GRID_W], [B_AXIAL_THETA, B_AXIAL_THETA],
                            HEAD_DIM // 4, HEAD_DIM, B_KV_HEADS)
    head_of = np.arange(B_Q_WIDTH) // HEAD_DIM
    ones_blk = jnp.asarray(head_of[:, None] == head_of[None, :], jnp.bfloat16)

    bf = jnp.bfloat16
    w_in, w_branch_a, w_branch_b, w_branch_c, w_out, w_up, w_down = (
        w.astype(bf) for w in (w_in, w_branch_a, w_branch_b, w_branch_c, w_out, w_up, w_down))
    c_bias = _c_bias_tables(rpb_c, rows)
    x2 = x.reshape(batch * seq, d)
    for layer in range(depth):
        qg = jnp.tile(q_norm_b[layer], B_Q_HEADS)[None, :]
        kg = jnp.tile(k_norm_b[layer], B_KV_HEADS)[None, :]
        a1, a4, a16, qbt, kb, vbt, qc, kc, vc = _in_proj(
            x2, w_in, layer, tabs_a, tabs_b, qg, kg, ones_blk, batch, seq)
        oa = _attn_a((a1, a4, a16), batch, seq)
        obt = _attn_b(qbt, kb, vbt, batch, seq)
        oc = _attn_c(qc, kc, vc, c_bias, layer, batch, seq)
        x2 = _mix(x2, oa, obt, oc, w_in, b_gate[layer][None, :],
                  w_branch_a, w_branch_b, w_branch_c, w_out,
                  ln1_g[layer][None, :], ln1_b[layer][None, :], layer, alpha, seq)
        x2 = _mlp(x2, w_up, w_down, ln2_g[layer][None, :], ln2_b[layer][None, :], layer, alpha)
    return x2.reshape(batch, seq, d)
```

```python
import functools
import math

import numpy as np
import jax
import jax.numpy as jnp
from jax import lax
from jax.experimental import pallas as pl
from jax.experimental.pallas import tpu as pltpu

HEAD_DIM = 64
A_HEADS = 4
A_CONFIGS = ((128, 1), (512, 4), (2048, 16))
A_ROPE_DIMS = HEAD_DIM // 4
A_ROPE_THETA = 500000.0
B_Q_HEADS = 8
B_KV_HEADS = 2
B_AXIAL_THETA = 10000.0
C_HEADS = 4
C_ROWS_MAX = 8
C_COLS = 16
GRID_W = 64
Q_BLOCK = 128
LN_EPS = 1e-5
RMS_EPS = 1e-6
NEG_INF = -1e30
LOG2E = math.log2(math.e)

A_WIDTH = A_HEADS * HEAD_DIM
B_Q_WIDTH = B_Q_HEADS * HEAD_DIM
B_KV_WIDTH = B_KV_HEADS * HEAD_DIM
C_WIDTH = C_HEADS * HEAD_DIM
QKV_COLS = 3 * A_WIDTH + B_Q_WIDTH + 2 * B_KV_WIDTH + 3 * C_WIDTH

VMEM_LIMIT_BYTES = 56 * 1024 * 1024
ROW_TILE = 1024
B_TILE = 256
B_VT_ROWS = HEAD_DIM + 16
B_QTILE = 512
B_UNROLL = 16
C_TILE_ROWS = 4
A_RADIUS = A_CONFIGS[0][0] // (2 * A_CONFIGS[0][1])
A_STEP = Q_BLOCK * max(dil for _, dil in A_CONFIGS)
A_INTERLEAVE = 4
A_GROUPS_PER_TRIP = 4

_NT = (((1,), (1,)), ((), ()))
_TN = (((0,), (0,)), ((), ()))


def _params(*sem):
    return pltpu.CompilerParams(dimension_semantics=sem, vmem_limit_bytes=VMEM_LIMIT_BYTES)


def _bf16(x):
    return x.astype(jnp.bfloat16)


def _dot(a, b):
    return jnp.dot(a, b, preferred_element_type=jnp.float32)


def _layer_norm(y, g, b):
    mu = jnp.mean(y, axis=-1, keepdims=True)
    d = y - mu
    var = jnp.mean(d * d, axis=-1, keepdims=True)
    return d * lax.rsqrt(var + LN_EPS) * g + b


def _rotate(x, c, sa, sb, half):
    w = x.shape[-1]
    return x * c + pltpu.roll(x, w - half, 1) * sa + pltpu.roll(x, half, 1) * sb


def _head_mean_sq(h, ones_blk):
    sq = h * h
    hi = _bf16(sq)
    lo = _bf16(sq - hi.astype(jnp.float32))
    return (_dot(hi, ones_blk) + _dot(lo, ones_blk)) * (1.0 / HEAD_DIM)


def _in_proj_kernel(x_ref, w_ref, ca_ref, saa_ref, sba_ref, cb_ref, sab_ref, sbb_ref,
                    qg_ref, kg_ref, ones_ref,
                    a1_ref, a4_ref, a16_ref, qbt_ref, kb_ref, vbt_ref, qc_ref, kc_ref, vc_ref,
                    stage_sc):
    xb = _bf16(x_ref[...])
    scale = HEAD_DIM ** -0.5

    off_b = 3 * A_WIDTH
    off_c = off_b + B_Q_WIDTH + 2 * B_KV_WIDTH

    ha = _dot(xb, w_ref[:, 0:off_b])
    ca, saa, sba = ca_ref[...], saa_ref[...], sba_ref[...]
    half_a = A_ROPE_DIMS // 2
    qa = _rotate(ha[:, 0:A_WIDTH], ca, saa, sba, half_a) * (scale * LOG2E)
    ka = _rotate(ha[:, A_WIDTH:2 * A_WIDTH], ca, saa, sba, half_a)
    qkv_a = (qa, ka, ha[:, 2 * A_WIDTH:3 * A_WIDTH])
    for part, val in enumerate(qkv_a):
        a1_ref[0, :, part * A_WIDTH:(part + 1) * A_WIDTH] = _bf16(val)
        for s in range(A_WIDTH // 128):
            stage_sc[part * (A_WIDTH // 128) + s] = val[:, s * 128:(s + 1) * 128]
    tm = x_ref.shape[0]
    for out_ref in (a4_ref, a16_ref):
        dil = out_ref.shape[1]
        for r in range(dil):
            for s in range(stage_sc.shape[0]):
                out_ref[0, r, :, s * 128:(s + 1) * 128] = _bf16(
                    stage_sc[s, pl.ds(r, tm // dil, stride=dil), :])

    hb = _dot(xb, w_ref[:, off_b:off_c])
    ones_q = ones_ref[...]
    ones_k = ones_ref[0:B_KV_WIDTH, 0:B_KV_WIDTH]
    cb, sab, sbb = cb_ref[...], sab_ref[...], sbb_ref[...]
    rep = B_Q_WIDTH // B_KV_WIDTH
    cb_q = jnp.concatenate([cb] * rep, axis=1)
    sab_q = jnp.concatenate([sab] * rep, axis=1)
    sbb_q = jnp.concatenate([sbb] * rep, axis=1)
    half_b = HEAD_DIM // 4
    qb = hb[:, 0:B_Q_WIDTH]
    qb = qb * lax.rsqrt(_head_mean_sq(qb, ones_q) + RMS_EPS) * qg_ref[...]
    qb = _rotate(qb, cb_q, sab_q, sbb_q, half_b) * (scale * LOG2E)
    kb = hb[:, B_Q_WIDTH:B_Q_WIDTH + B_KV_WIDTH]
    kb = kb * lax.rsqrt(_head_mean_sq(kb, ones_k) + RMS_EPS) * kg_ref[...]
    kb = _rotate(kb, cb, sab, sbb, half_b)
    vb = hb[:, B_Q_WIDTH + B_KV_WIDTH:]
    qbt_ref[0] = _bf16(qb.T)
    kb_ref[...] = _bf16(kb)
    ones_rows = jnp.ones((B_VT_ROWS - HEAD_DIM, B_TILE), jnp.bfloat16)
    for c in range(vbt_ref.shape[1]):
        vt = _bf16(vb[c * B_TILE:(c + 1) * B_TILE, :].T)
        for g in range(B_KV_HEADS):
            vbt_ref[0, c, g * B_VT_ROWS:g * B_VT_ROWS + HEAD_DIM, :] = vt[g * HEAD_DIM:(g + 1) * HEAD_DIM]
            vbt_ref[0, c, g * B_VT_ROWS + HEAD_DIM:(g + 1) * B_VT_ROWS, :] = ones_rows

    hc = _dot(xb, w_ref[:, off_c:off_c + 3 * C_WIDTH])
    qc_ref[...] = _bf16(hc[:, 0:C_WIDTH] * (scale * LOG2E))
    kc_ref[...] = _bf16(hc[:, C_WIDTH:2 * C_WIDTH])
    vc_ref[...] = _bf16(hc[:, 2 * C_WIDTH:3 * C_WIDTH])


def _layer_spec(stacked, layer):
    zeros = (0,) * (stacked.ndim - 1)
    return pl.BlockSpec((None,) + stacked.shape[1:], lambda *_: (layer,) + zeros,
                        pipeline_mode=pl.Buffered(1))


def _in_proj(x2, w_in, layer, tabs_a, tabs_b, qg, kg, ones_blk, batch, seq):
    n, d = x2.shape
    tm = ROW_TILE
    nt = seq // tm
    row = lambda i: (i, 0)
    pos = lambda i: (i % nt, 0)
    const = lambda i: (0, 0)
    bf = jnp.bfloat16
    out_shape = (
        jax.ShapeDtypeStruct((batch, seq, 3 * A_WIDTH), bf),
        *(jax.ShapeDtypeStruct((batch, dil, seq // dil, 3 * A_WIDTH), bf) for _, dil in A_CONFIGS[1:]),
        jax.ShapeDtypeStruct((batch, B_Q_WIDTH, seq), bf),
        jax.ShapeDtypeStruct((n, B_KV_WIDTH), bf),
        jax.ShapeDtypeStruct((batch, seq // B_TILE, B_KV_HEADS * B_VT_ROWS, B_TILE), bf),
        jax.ShapeDtypeStruct((n, C_WIDTH), bf), jax.ShapeDtypeStruct((n, C_WIDTH), bf),
        jax.ShapeDtypeStruct((n, C_WIDTH), bf),
    )
    out_specs = (
        pl.BlockSpec((1, tm, 3 * A_WIDTH), lambda i: (i // nt, i % nt, 0)),
        *(pl.BlockSpec((1, dil, tm // dil, 3 * A_WIDTH), lambda i: (i // nt, 0, i % nt, 0))
          for _, dil in A_CONFIGS[1:]),
        pl.BlockSpec((1, B_Q_WIDTH, tm), lambda i: (i // nt, 0, i % nt)),
        pl.BlockSpec((tm, B_KV_WIDTH), row),
        pl.BlockSpec((1, tm // B_TILE, B_KV_HEADS * B_VT_ROWS, B_TILE),
                     lambda i: (i // nt, i % nt, 0, 0)),
        pl.BlockSpec((tm, C_WIDTH), row), pl.BlockSpec((tm, C_WIDTH), row),
        pl.BlockSpec((tm, C_WIDTH), row),
    )
    in_specs = [
        pl.BlockSpec((tm, d), row),
        _layer_spec(w_in, layer),
        pl.BlockSpec((tm, A_WIDTH), pos), pl.BlockSpec((tm, A_WIDTH), pos),
        pl.BlockSpec((tm, A_WIDTH), pos),
        pl.BlockSpec((tm, B_KV_WIDTH), pos), pl.BlockSpec((tm, B_KV_WIDTH), pos),
        pl.BlockSpec((tm, B_KV_WIDTH), pos),
        pl.BlockSpec(qg.shape, const), pl.BlockSpec(kg.shape, const),
        pl.BlockSpec(ones_blk.shape, const),
    ]
    return pl.pallas_call(
        _in_proj_kernel, grid=(n // tm,), in_specs=in_specs, out_specs=out_specs,
        out_shape=out_shape,
        scratch_shapes=[pltpu.VMEM((3 * A_WIDTH // 128, tm, 128), jnp.float32)],
        compiler_params=_params("parallel"), name="in_proj",
    )(x2, w_in, *tabs_a, *tabs_b, qg, kg, ones_blk)


def _attn_a_kernel(*refs, seq):
    n_cfg = len(A_CONFIGS)
    qkv_refs = [refs[3 * c:3 * c + 3] for c in range(n_cfg)]
    bias_ref, hmask_ref, o_ref, o_sc, m_sc, l_sc = refs[3 * n_cfg:]
    i = pl.program_id(1)
    span = Q_BLOCK + 2 * A_RADIUS
    lane_head = lax.broadcasted_iota(jnp.int32, (Q_BLOCK, A_WIDTH), 1) // HEAD_DIM

    def widen(t, width):
        return t if t.shape[1] in (1, width) else jnp.concatenate([t] * (width // t.shape[1]), axis=1)

    def per_head_lanes(stacked):
        tiles = [widen(stacked[h * Q_BLOCK:(h + 1) * Q_BLOCK], A_WIDTH) for h in range(A_HEADS)]
        out = tiles[-1]
        for h in range(A_HEADS - 2, -1, -1):
            out = jnp.where(lane_head == h, tiles[h], out)
        return out

    def scores(blk):
        halves = []
        for h0 in range(0, A_HEADS, A_HEADS // 2):
            qm = jnp.concatenate([blk["q"] * hmask_ref[h] for h in range(h0, h0 + A_HEADS // 2)], axis=0)
            halves.append(lax.dot_general(qm, blk["kw"], _NT, preferred_element_type=jnp.float32))
        return jnp.concatenate(halves, axis=0)

    def softmax(blk, s, first):
        rows = blk["rows"]
        s = s + jnp.concatenate([bias_ref[blk["edge"]]] * A_HEADS, axis=0)
        m_new = jnp.max(s, axis=-1, keepdims=True)
        alpha = None
        if not first:
            m_prev = jnp.concatenate([m_sc[h, rows, :] for h in range(A_HEADS)], axis=0)
            l_prev = jnp.concatenate([l_sc[h, rows, :] for h in range(A_HEADS)], axis=0)
            m_new = jnp.maximum(m_prev, m_new)
            alpha = jnp.exp2(m_prev - m_new)
        p = jnp.exp2(s - widen(m_new, span))
        l_new = jnp.sum(p, axis=-1, keepdims=True)
        if not first:
            l_new = l_new + alpha * l_prev
        return _bf16(p), m_new, l_new, alpha

    def update(blk, pv, m_new, l_new, alpha, first, last):
        rows = blk["rows"]
        o_new = per_head_lanes(pv)
        if not first:
            o_prev = jnp.concatenate([o_sc[s, rows, :] for s in range(o_sc.shape[0])], axis=1)
            o_new = o_new + per_head_lanes(alpha) * o_prev
        if last:
            o_new = o_new / per_head_lanes(l_new)
        else:
            for h in range(A_HEADS):
                hs = slice(h * Q_BLOCK, (h + 1) * Q_BLOCK)
                m_sc[h, rows, :] = jnp.broadcast_to(m_new[hs], (Q_BLOCK, 128))
                l_sc[h, rows, :] = jnp.broadcast_to(l_new[hs], (Q_BLOCK, 128))
        for s in range(o_sc.shape[0]):
            o_sc[s, rows, :] = o_new[:, s * 128:(s + 1) * 128]

    def process(blocks, first, last):
        ss = [scores(blk) for blk in blocks]
        res = []
        for blk, s in zip(blocks, ss):
            p, m_new, l_new, alpha = softmax(blk, s, first)
            half = p.shape[0] // 2
            pv = jnp.concatenate([_dot(p[:half], blk["vw"]), _dot(p[half:], blk["vw"])], axis=0)
            res.append((pv, m_new, l_new, alpha))
        for blk, (pv, m_new, l_new, alpha) in zip(blocks, res):
            update(blk, pv, m_new, l_new, alpha, first, last)

    for ci, (_, dil) in enumerate(A_CONFIGS):
        q_ref, k_ref, v_ref = qkv_refs[ci]
        sub_len = seq // dil
        blocks_per_res = A_STEP // (dil * Q_BLOCK)
        first, last = ci == 0, ci == n_cfg - 1

        def block(idx, dil=dil, q_ref=q_ref, k_ref=k_ref, v_ref=v_ref, sub_len=sub_len,
                  blocks_per_res=blocks_per_res):
            r, qb = idx // blocks_per_res, idx % blocks_per_res
            base = i * (A_STEP // dil) + qb * Q_BLOCK
            start = pl.multiple_of(jnp.clip(base - A_RADIUS, 0, sub_len - span), A_RADIUS)
            q_rows = pl.ds(pl.multiple_of(qb * Q_BLOCK, Q_BLOCK), Q_BLOCK)
            if dil == 1:
                q, kw, vw = q_ref[0, q_rows, :], k_ref[0, pl.ds(start, span), :], v_ref[0, pl.ds(start, span), :]
                rows = q_rows
            else:
                q, kw, vw = (q_ref[0, r, q_rows, :], k_ref[0, r, pl.ds(start, span), :],
                             v_ref[0, r, pl.ds(start, span), :])
                rows = pl.ds(qb * Q_BLOCK * dil + r, Q_BLOCK, stride=dil)
            return dict(q=q, kw=kw, vw=vw, rows=rows, edge=(base - start) // A_RADIUS)

        def trip(t, carry, block=block, first=first, last=last):
            for u in range(A_GROUPS_PER_TRIP):
                first_blk = A_INTERLEAVE * (A_GROUPS_PER_TRIP * t + u)
                process([block(first_blk + v) for v in range(A_INTERLEAVE)], first, last)
            return carry

        lax.fori_loop(0, A_STEP // Q_BLOCK // (A_INTERLEAVE * A_GROUPS_PER_TRIP), trip, 0)

    o_ref[0] = _bf16(jnp.concatenate([o_sc[s] for s in range(o_sc.shape[0])], axis=1))


def _attn_a(a_qkv, batch, seq):
    single = pl.Buffered(1)
    args, in_specs = [], []
    for (_, dil), arr in zip(A_CONFIGS, a_qkv):
        sub_len = seq // dil
        if dil == 1:
            q_spec = pl.BlockSpec((1, A_STEP, A_WIDTH), lambda b, i: (b, i, 0))
            kv_spec = lambda part: pl.BlockSpec((1, seq, A_WIDTH), lambda b, i: (b, 0, part),
                                                pipeline_mode=single)
        else:
            q_spec = pl.BlockSpec((1, dil, A_STEP // dil, A_WIDTH), lambda b, i: (b, 0, i, 0))
            kv_spec = lambda part, dil=dil, sub_len=sub_len: pl.BlockSpec(
                (1, dil, sub_len, A_WIDTH), lambda b, i: (b, 0, 0, part), pipeline_mode=single)
        args += [arr, arr, arr]
        in_specs += [q_spec, kv_spec(1), kv_spec(2)]
    span = Q_BLOCK + 2 * A_RADIUS
    rel = np.arange(Q_BLOCK)[:, None] - np.arange(span)[None, :]
    bias = jnp.asarray(np.stack([np.where(np.abs(rel + e * A_RADIUS) <= A_RADIUS, 0.0, NEG_INF)
                                 for e in range(3)]), jnp.float32)
    hmask = jnp.asarray((np.arange(A_WIDTH)[None, :] // HEAD_DIM == np.arange(A_HEADS)[:, None])[:, None, :],
                        jnp.bfloat16)
    const3 = lambda b, i: (0, 0, 0)
    out = pl.pallas_call(
        functools.partial(_attn_a_kernel, seq=seq), grid=(batch, seq // A_STEP),
        in_specs=in_specs + [pl.BlockSpec(bias.shape, const3), pl.BlockSpec(hmask.shape, const3)],
        out_specs=pl.BlockSpec((1, A_STEP, A_WIDTH), lambda b, i: (b, i, 0)),
        out_shape=jax.ShapeDtypeStruct((batch, seq, A_WIDTH), jnp.bfloat16),
        scratch_shapes=[pltpu.VMEM((A_WIDTH // 128, A_STEP, 128), jnp.float32),
                        pltpu.VMEM((A_HEADS, A_STEP, 128), jnp.float32),
                        pltpu.VMEM((A_HEADS, A_STEP, 128), jnp.float32)],
        compiler_params=_params("parallel", "arbitrary"), name="attn_a",
    )(*args, bias, hmask)
    return out.reshape(batch * seq, A_WIDTH)


def _attn_b_kernel(qt_ref, k_ref, vt_ref, ot_ref, wq_sc, m_sc, acc_sc, st_sc, mx_sc):
    assert B_UNROLL % 2 == 0 and vt_ref.shape[1] % B_UNROLL == 0
    n_kt = vt_ref.shape[1]
    group = B_Q_HEADS // B_KV_HEADS
    wq_sc[...] = jnp.zeros(wq_sc.shape, wq_sc.dtype)
    for h in range(B_Q_HEADS):
        g = h // group
        wq_sc[h, g * HEAD_DIM:(g + 1) * HEAD_DIM, :] = qt_ref[0, h * HEAD_DIM:(h + 1) * HEAD_DIM, :]
    m_sc[...] = jnp.full(m_sc.shape, NEG_INF, jnp.float32)
    acc_sc[...] = jnp.zeros(acc_sc.shape, jnp.float32)

    def k_tile(j):
        return k_ref[0, pl.ds(pl.multiple_of(j * B_TILE, B_TILE), B_TILE), :]

    def softmax_pv(j, slot, h):
        g = h // group
        vt = vt_ref[0, j, g * B_VT_ROWS:(g + 1) * B_VT_ROWS, :]
        m_prev = m_sc[h]
        m_new = jnp.maximum(m_prev, mx_sc[slot, h])
        alpha = jnp.exp2(m_prev - m_new)
        p = jnp.exp2(st_sc[slot, h] - m_new)
        acc_sc[h] = alpha * acc_sc[h] + _dot(vt, _bf16(p))
        m_sc[h] = m_new

    def score_tile(k, slot, h):
        st = _dot(k, wq_sc[h])
        st_sc[slot, h] = st
        mx_sc[slot, h] = jnp.max(st, axis=0, keepdims=True)

    for h in range(B_Q_HEADS):
        score_tile(k_tile(0), 0, h)

    def step(j, slot):
        k_next = k_tile(jnp.minimum(j + 1, n_kt - 1))
        for h in range(B_Q_HEADS):
            score_tile(k_next, 1 - slot, h)
            softmax_pv(j, slot, h)

    def body(jj, carry):
        for u in range(B_UNROLL):
            step(B_UNROLL * jj + u, u % 2)
        return carry

    lax.fori_loop(0, n_kt // B_UNROLL, body, 0)
    for h in range(B_Q_HEADS):
        ot_ref[0, h * HEAD_DIM:(h + 1) * HEAD_DIM, :] = _bf16(
            acc_sc[h, 0:HEAD_DIM] / acc_sc[h, HEAD_DIM:HEAD_DIM + 1])


def _attn_b(qbt, kb, vbt, batch, seq):
    tq = B_QTILE
    group = B_Q_HEADS // B_KV_HEADS
    kb3 = kb.reshape(batch, seq, B_KV_WIDTH)
    return pl.pallas_call(
        _attn_b_kernel, grid=(batch, seq // tq),
        in_specs=[
            pl.BlockSpec((1, B_Q_WIDTH, tq), lambda b, i: (b, 0, i)),
            pl.BlockSpec((1, seq, B_KV_WIDTH), lambda b, i: (b, 0, 0)),
            pl.BlockSpec((1, seq // B_TILE, B_KV_HEADS * B_VT_ROWS, B_TILE), lambda b, i: (b, 0, 0, 0)),
        ],
        out_specs=pl.BlockSpec((1, B_Q_WIDTH, tq), lambda b, i: (b, 0, i)),
        out_shape=jax.ShapeDtypeStruct((batch, B_Q_WIDTH, seq), jnp.bfloat16),
        scratch_shapes=[
            pltpu.VMEM((B_Q_HEADS, B_KV_WIDTH, tq), jnp.bfloat16),
            pltpu.VMEM((B_Q_HEADS, 1, tq), jnp.float32),
            pltpu.VMEM((B_Q_HEADS, B_VT_ROWS, tq), jnp.float32),
            pltpu.VMEM((2, B_Q_HEADS, B_TILE, tq), jnp.float32),
            pltpu.VMEM((2, B_Q_HEADS, 1, tq), jnp.float32),
        ],
        compiler_params=_params("parallel", "parallel"), name="attn_b",
    )(qbt, kb3, vbt)


def _c_window_rows():
    return C_TILE_ROWS + C_ROWS_MAX - 1


def _attn_c_kernel(q_ref, k_ref, v_ref, bias_ref, o_ref, *, rows):
    t = pl.program_id(1)
    win = _c_window_rows() * GRID_W
    ws = jnp.clip(t * C_TILE_ROWS - C_ROWS_MAX // 2, 0, rows - _c_window_rows())
    start = pl.multiple_of(ws * GRID_W, GRID_W)
    kw = k_ref[0, pl.ds(start, win), :]
    vw = v_ref[0, pl.ds(start, win), :]
    q = q_ref[0]
    lane_head = lax.broadcasted_iota(jnp.int32, q.shape, 1) // HEAD_DIM
    out = jnp.zeros(q.shape, jnp.float32)
    for h in range(C_HEADS):
        qh = jnp.where(lane_head == h, q, jnp.zeros_like(q))
        s = lax.dot_general(qh, kw, _NT, preferred_element_type=jnp.float32) + bias_ref[0, h]
        m = jnp.max(s, axis=-1, keepdims=True)
        p = jnp.exp2(s - m)
        l = jnp.sum(p, axis=-1, keepdims=True)
        pv = _dot(_bf16(p), vw)
        out = jnp.where(lane_head == h, pv / l, out)
    o_ref[0] = _bf16(out)


def _c_bias_tables(rpb, rows):
    n_tiles = rows // C_TILE_ROWS
    wr = _c_window_rows()
    kr = min(C_ROWS_MAX, rows)
    c = np.arange(GRID_W)
    c0 = np.clip(c - C_COLS // 2, 0, GRID_W - C_COLS)
    col_ok = (c[None, :] >= c0[:, None]) & (c[None, :] < c0[:, None] + C_COLS)
    pad = GRID_W - C_COLS
    padded = jnp.pad(rpb.astype(jnp.float32) * LOG2E, ((0, 0), (0, 0), (0, 0), (pad, pad)))
    shifted = jnp.stack([padded[..., GRID_W - 1 - qc:2 * GRID_W - 1 - qc] for qc in range(GRID_W)],
                        axis=-2)
    by_dr = jnp.where(col_ok, shifted, NEG_INF)
    masked = jnp.full(by_dr.shape[:2] + (GRID_W, GRID_W), NEG_INF, jnp.float32)
    tabs = []
    for t in (0, 1, n_tiles - 1):
        ws = int(np.clip(t * C_TILE_ROWS - C_ROWS_MAX // 2, 0, rows - wr))
        q_blocks = []
        for r in range(t * C_TILE_ROWS, (t + 1) * C_TILE_ROWS):
            r0 = int(np.clip(r - kr // 2, 0, rows - kr))
            k_blocks = [by_dr[:, :, krow - r + C_ROWS_MAX - 1] if r0 <= krow < r0 + kr else masked
                        for krow in range(ws, ws + wr)]
            q_blocks.append(jnp.concatenate(k_blocks, axis=-1))
        tabs.append(jnp.concatenate(q_blocks, axis=-2))
    return jnp.stack(tabs, axis=1)


def _attn_c(qc, kc, vc, bias_tabs, layer, batch, seq):
    rows = seq // GRID_W
    n_tiles = rows // C_TILE_ROWS
    tq = C_TILE_ROWS * GRID_W
    view = lambda t: t.reshape(batch, seq, C_WIDTH)
    cls = lambda b, t: (layer, jnp.where(t == 0, 0, jnp.where(t == n_tiles - 1, 2, 1)), 0, 0, 0)
    out = pl.pallas_call(
        functools.partial(_attn_c_kernel, rows=rows), grid=(batch, n_tiles),
        in_specs=[
            pl.BlockSpec((1, tq, C_WIDTH), lambda b, t: (b, t, 0)),
            pl.BlockSpec((1, seq, C_WIDTH), lambda b, t: (b, 0, 0)),
            pl.BlockSpec((1, seq, C_WIDTH), lambda b, t: (b, 0, 0)),
            pl.BlockSpec((None, 1) + bias_tabs.shape[2:], cls),
        ],
        out_specs=pl.BlockSpec((1, tq, C_WIDTH), lambda b, t: (b, t, 0)),
        out_shape=jax.ShapeDtypeStruct((batch, seq, C_WIDTH), jnp.bfloat16),
        compiler_params=_params("parallel", "arbitrary"), name="attn_c",
    )(view(qc), view(kc), view(vc), bias_tabs)
    return out.reshape(batch * seq, C_WIDTH)


def _mix_kernel(x_ref, oa_ref, obt_ref, oc_ref, wg_ref, bg_ref, wa_ref, wb_ref, wc_ref, wo_ref,
                g_ref, b_ref, y_ref, *, alpha):
    x = x_ref[...]
    xb = _bf16(x)
    d = x.shape[1]
    projs = (
        _dot(oa_ref[...], wa_ref[...]),
        lax.dot_general(obt_ref[0], wb_ref[...], _TN, preferred_element_type=jnp.float32),
        _dot(oc_ref[...], wc_ref[...]),
    )
    merged = None
    for br, proj in enumerate(projs):
        cols = slice(QKV_COLS + br * d, QKV_COLS + (br + 1) * d)
        logits = _dot(xb, wg_ref[:, cols]) + bg_ref[:, br * d:(br + 1) * d]
        term = jax.nn.sigmoid(logits) * proj
        merged = term if merged is None else merged + term
    mix = _dot(_bf16(merged), wo_ref[...])
    y_ref[...] = _layer_norm(alpha * x + mix, g_ref[...], b_ref[...])


def _mix(x2, oa, obt, oc, w_in, bg, wa, wb, wc, wo, g, b, layer, alpha, seq):
    n, d = x2.shape
    tm = ROW_TILE
    nt = seq // tm
    row = lambda i: (i, 0)
    const = lambda i: (0, 0)
    full = lambda a: pl.BlockSpec(a.shape, const)
    return pl.pallas_call(
        functools.partial(_mix_kernel, alpha=alpha), grid=(n // tm,),
        in_specs=[
            pl.BlockSpec((tm, d), row), pl.BlockSpec((tm, A_WIDTH), row),
            pl.BlockSpec((1, B_Q_WIDTH, tm), lambda i: (i // nt, 0, i % nt)),
            pl.BlockSpec((tm, C_WIDTH), row),
            _layer_spec(w_in, layer), full(bg), _layer_spec(wa, layer), _layer_spec(wb, layer),
            _layer_spec(wc, layer), _layer_spec(wo, layer), full(g), full(b),
        ],
        out_specs=pl.BlockSpec((tm, d), row),
        out_shape=jax.ShapeDtypeStruct((n, d), jnp.float32),
        compiler_params=_params("parallel"), name="mix",
    )(x2, oa, obt, oc, w_in, bg, wa, wb, wc, wo, g, b)


def _mlp_kernel(x_ref, wu_ref, wd_ref, g_ref, b_ref, y_ref, *, alpha, chunk):
    x = x_ref[...]
    xb = _bf16(x)
    ff = None
    for c in range(wu_ref.shape[1] // chunk):
        hid = jnp.square(jnp.maximum(_dot(xb, wu_ref[:, c * chunk:(c + 1) * chunk]), 0.0))
        part = _dot(_bf16(hid), wd_ref[c * chunk:(c + 1) * chunk, :])
        ff = part if ff is None else ff + part
    y_ref[...] = _layer_norm(alpha * x + ff, g_ref[...], b_ref[...])


def _mlp(x2, wu, wd, g, b, layer, alpha):
    n, d = x2.shape
    tm = ROW_TILE
    row = lambda i: (i, 0)
    const = lambda i: (0, 0)
    return pl.pallas_call(
        functools.partial(_mlp_kernel, alpha=alpha, chunk=1024), grid=(n // tm,),
        in_specs=[
            pl.BlockSpec((tm, d), row),
            _layer_spec(wu, layer), _layer_spec(wd, layer),
            pl.BlockSpec(g.shape, const), pl.BlockSpec(b.shape, const),
        ],
        out_specs=pl.BlockSpec((tm, d), row),
        out_shape=jax.ShapeDtypeStruct((n, d), jnp.float32),
        compiler_params=_params("parallel"), name="mlp",
    )(x2, wu, wd, g, b)


def _rotary_tables(pos_list, theta_list, half, width_per_head, reps):
    s = pos_list[0].shape[0]
    c = jnp.ones((s, width_per_head), jnp.float32)
    sa = jnp.zeros((s, width_per_head), jnp.float32)
    sb = jnp.zeros((s, width_per_head), jnp.float32)
    for i, (pos, theta) in enumerate(zip(pos_list, theta_list)):
        inv = theta ** (-jnp.arange(half, dtype=jnp.float32) / half)
        ang = pos.astype(jnp.float32)[:, None] * inv[None, :]
        cos, sin = jnp.cos(ang), jnp.sin(ang)
        o = 2 * half * i
        c = c.at[:, o:o + half].set(cos).at[:, o + half:o + 2 * half].set(cos)
        sa = sa.at[:, o:o + half].set(-sin)
        sb = sb.at[:, o + half:o + 2 * half].set(sin)
    tile = lambda t: jnp.tile(t, (1, reps))
    return tile(c), tile(sa), tile(sb)


def kernel(x, w_in, b_gate, q_norm_b, k_norm_b, rpb_c, w_branch_a, w_branch_b, w_branch_c, w_out,
           ln1_g, ln1_b, w_up, w_down, ln2_g, ln2_b):
    batch, seq, d = x.shape
    depth = w_in.shape[0]
    alpha = (2 * depth) ** 0.25
    rows = seq // GRID_W
    assert seq % ROW_TILE == 0 and seq % B_QTILE == 0 and (seq // B_TILE) % B_UNROLL == 0
    assert rows % C_TILE_ROWS == 0 and rows >= 2 * _c_window_rows()
    assert all(window // (2 * dil) == A_RADIUS and (seq // dil) >= Q_BLOCK + 2 * A_RADIUS
               for window, dil in A_CONFIGS)
    assert A_CONFIGS[0][1] == 1 and seq % A_STEP == 0 and len(A_CONFIGS) == 3

    pos = jnp.arange(seq)
    tabs_a = _rotary_tables([pos], [A_ROPE_THETA], A_ROPE_DIMS // 2, HEAD_DIM, A_HEADS)
    tabs_b = _rotary_tables([pos // GRID_W, pos % GRID_W], [B_AXIAL_THETA, B_AXIAL_THETA],
                            HEAD_DIM // 4, HEAD_DIM, B_KV_HEADS)
    head_of = np.arange(B_Q_WIDTH) // HEAD_DIM
    ones_blk = jnp.asarray(head_of[:, None] == head_of[None, :], jnp.bfloat16)

    bf = jnp.bfloat16
    w_in, w_branch_a, w_branch_b, w_branch_c, w_out, w_up, w_down = (
        w.astype(bf) for w in (w_in, w_branch_a, w_branch_b, w_branch_c, w_out, w_up, w_down))
    c_bias = _c_bias_tables(rpb_c, rows)
    x2 = x.reshape(batch * seq, d)
    for layer in range(depth):
        qg = jnp.tile(q_norm_b[layer], B_Q_HEADS)[None, :]
        kg = jnp.tile(k_norm_b[layer], B_KV_HEADS)[None, :]
        a1, a4, a16, qbt, kb, vbt, qc, kc, vc = _in_proj(
            x2, w_in, layer, tabs_a, tabs_b, qg, kg, ones_blk, batch, seq)
        oa = _attn_a((a1, a4, a16), batch, seq)
        obt = _attn_b(qbt, kb, vbt, batch, seq)
        oc = _attn_c(qc, kc, vc, c_bias, layer, batch, seq)
        x2 = _mix(x2, oa, obt, oc, w_in, b_gate[layer][None, :],
                  w_branch_a, w_branch_b, w_branch_c, w_out,
                  ln1_g[layer][None, :], ln1_b[layer][None, :], layer, alpha, seq)
        x2 = _mlp(x2, w_up, w_down, ln2_g[layer][None, :], ln2_b[layer][None, :], layer, alpha)
    return x2.reshape(batch, seq, d)
```
